```python
import math
import jax, jax.numpy as jnp
from jax import lax
import numpy as np

D_MODEL = 1024
BATCH = 16
SEQ = 256
DEPTH = 4
DEC_BATCH = 8
DEC_SEQ = 2048
PAST_LEN = 256

GRID_W = 64
N_MIXERS = 2
N_RG = (DEPTH + 1) // 2
N_MLA = DEPTH // 2
D_FF = 4 * D_MODEL
D_RNN = D_MODEL
RG_BLOCKS = 8
RG_BS = D_RNN // RG_BLOCKS
RG_C = 8.0
CONV_W = 4
CONV_LEFT = (CONV_W - 1) // 2
N_HEADS = 8
QK_NOPE = 128
QK_ROPE = 64
V_DIM = 128
Q_LORA = 512
KV_LORA = 256
ROPE_THETA = 10000.0
Q_BLOCK = 128
EPS = 1e-6

kernel_name = "hybrid_rglru_mla_diffusion_step"


def rms_norm(x, g):
    xf = x.astype(jnp.float32)
    y = xf * lax.rsqrt(jnp.mean(xf * xf, axis=-1, keepdims=True) + EPS)
    return (y * g.astype(jnp.float32)).astype(x.dtype)


def adaln_params(cond, w_ada, b_ada):
    mod = jax.nn.silu(cond) @ w_ada + b_ada
    return jnp.split(mod, 6, axis=-1)


def modulate(h, shift, scale):
    return h * (1 + scale[:, None, :]) + shift[:, None, :]


def sq_relu_mlp(h, w1, w2):
    a = jax.nn.relu(h @ w1)
    return (a * a) @ w2


def depthwise_conv(x, w, b):
    T = x.shape[1]
    xp = jnp.pad(x, ((0, 0), (CONV_LEFT, CONV_W - 1 - CONV_LEFT), (0, 0)))
    out = xp[:, 0:T] * w[0]
    for k in range(1, CONV_W):
        out = out + xp[:, k:k + T] * w[k]
    return out + b


def linear_scan(a, b, h0, reverse):
    idx = -1 if reverse else 0
    b = b.at[:, idx].add(a[:, idx] * h0)

    def combine(e1, e2):
        a1, b1 = e1
        a2, b2 = e2
        return a1 * a2, a2 * b1 + b2

    _, h = lax.associative_scan(combine, (a, b), axis=1, reverse=reverse)
    return h


def rglru_block(h, h0, w_in, conv_w, conv_b, wa, ba, wx, bx, lam, w_out):
    B, T, _ = h.shape
    xb, yb = jnp.split(h @ w_in, 2, axis=-1)
    yb = jax.nn.gelu(yb)
    xb = depthwise_conv(xb, conv_w, conv_b)
    xf = xb.astype(jnp.float32)
    xblk = xf.reshape(B, T, RG_BLOCKS, RG_BS)
    r = jnp.einsum('btnk,rnkj->rbtnj', xblk, wa.astype(jnp.float32)).reshape(2, B, T, D_RNN)
    i = jnp.einsum('btnk,rnkj->rbtnj', xblk, wx.astype(jnp.float32)).reshape(2, B, T, D_RNN)
    r = jax.nn.sigmoid(r + ba.astype(jnp.float32)[:, None, None, :])
    i = jax.nn.sigmoid(i + bx.astype(jnp.float32)[:, None, None, :])
    log_a = -RG_C * r * jax.nn.softplus(-lam.astype(jnp.float32))[:, None, None, :]
    a = jnp.exp(log_a)
    b = jnp.sqrt(-jnp.expm1(2.0 * log_a)) * (i * xf[None])
    h0f = h0.astype(jnp.float32)
    h_f = linear_scan(a[0], b[0], h0f[:, 0], reverse=False)
    h_b = linear_scan(a[1], b[1], h0f[:, 1], reverse=True)
    final = jnp.stack([h_f[:, -1], h_b[:, 0]], axis=1).astype(h.dtype)
    out = ((h_f + h_b).astype(h.dtype) * yb) @ w_out
    return out, final


def axial_rope_tables(n):
    rows = n // GRID_W
    row = jnp.repeat(jnp.arange(rows, dtype=jnp.float32), GRID_W)
    col = jnp.tile(jnp.arange(GRID_W, dtype=jnp.float32), rows)
    half = QK_ROPE // 2
    inv = ROPE_THETA ** (-jnp.arange(0, half, 2, dtype=jnp.float32) / half)
    ang = jnp.concatenate([row[:, None] * inv, col[:, None] * inv], axis=-1)
    return jnp.cos(ang), jnp.sin(ang)


def apply_rope(x, cos, sin):
    xf = x.astype(jnp.float32).reshape(x.shape[:-1] + (QK_ROPE // 2, 2))
    x1, x2 = xf[..., 0], xf[..., 1]
    out = jnp.stack([x1 * cos - x2 * sin, x1 * sin + x2 * cos], axis=-1)
    return out.reshape(x.shape).astype(x.dtype)


def mla_project(h, w_dqkv, g_q, g_kv, w_uq):
    B, T, _ = h.shape
    proj = h @ w_dqkv
    cq = rms_norm(proj[..., :Q_LORA], g_q)
    ckv = rms_norm(proj[..., Q_LORA:Q_LORA + KV_LORA], g_kv)
    kpe = proj[..., Q_LORA + KV_LORA:]
    q = (cq @ w_uq).reshape(B, T, N_HEADS, QK_NOPE + QK_ROPE)
    return q[..., :QK_NOPE], q[..., QK_NOPE:], ckv, kpe


def mla_kv(ckv, w_ukv):
    B, T, _ = ckv.shape
    kv = (ckv @ w_ukv).reshape(B, T, N_HEADS, QK_NOPE + V_DIM)
    return kv[..., :QK_NOPE], kv[..., QK_NOPE:]


def block_attention(q_nope, q_pe, k_nope, k_pe, v):
    B, S = q_nope.shape[:2]
    nb = S // Q_BLOCK
    qn = q_nope.reshape(B, nb, Q_BLOCK, N_HEADS, QK_NOPE).transpose(1, 0, 2, 3, 4)
    qp = q_pe.reshape(B, nb, Q_BLOCK, N_HEADS, QK_ROPE).transpose(1, 0, 2, 3, 4)
    scale = (QK_NOPE + QK_ROPE) ** -0.5

    def one_block(args):
        qn_b, qp_b = args
        s = (jnp.einsum('bqhd,bkhd->bhqk', qn_b, k_nope)
             + jnp.einsum('bqhd,bkd->bhqk', qp_b, k_pe))
        p = jax.nn.softmax(s.astype(jnp.float32) * scale, axis=-1).astype(v.dtype)
        return jnp.einsum('bhqk,bkhd->bqhd', p, v)

    o = lax.map(one_block, (qn, qp))
    return o.transpose(1, 0, 2, 3, 4).reshape(B, S, N_HEADS * V_DIM)


def setup_inputs(seed: int = 0) -> dict:
    key = jax.random.key(seed)
    ks = iter(jax.random.split(key, 40))
    f32 = jnp.float32

    def nrm(shape, scale):
        return jax.random.normal(next(ks), shape, f32) * scale

    u = jax.random.uniform(next(ks), (N_RG, 2, D_RNN), f32, 0.9, 0.999)
    rg_lambda = jnp.log(u) - jnp.log1p(-u)
    return {
        "x_prompt": nrm((BATCH, SEQ, D_MODEL), 1.0),
        "x_sample": nrm((DEC_BATCH, DEC_SEQ, D_MODEL), 1.0),
        "state_rglru": nrm((DEC_BATCH, N_RG, 2, D_RNN), 0.5),
        "cache_ckv": nrm((DEC_BATCH, N_MLA, PAST_LEN, KV_LORA), 1.0),
        "cache_kpe": nrm((DEC_BATCH, N_MLA, PAST_LEN, QK_ROPE), 1.0),
        "c": nrm((DEC_BATCH, D_MODEL), 1.0),
        "c_ctx": nrm((D_MODEL,), 1.0),
        "ada_w": nrm((DEPTH, D_MODEL, 6 * D_MODEL), 0.5 * D_MODEL ** -0.5),
        "ada_b": nrm((DEPTH, 6 * D_MODEL), 0.02),
        "norm_mix": 1.0 + nrm((DEPTH, D_MODEL), 0.02),
        "norm_mlp": 1.0 + nrm((DEPTH, D_MODEL), 0.02),
        "mlp_w1": nrm((DEPTH, D_MODEL, D_FF), D_MODEL ** -0.5),
        "mlp_w2": nrm((DEPTH, D_FF, D_MODEL), D_FF ** -0.5),
        "rg_w_in": nrm((N_RG, D_MODEL, 2 * D_RNN), D_MODEL ** -0.5),
        "rg_conv_w": nrm((N_RG, CONV_W, D_RNN), CONV_W ** -0.5),
        "rg_conv_b": nrm((N_RG, D_RNN), 0.01),
        "rg_wa": nrm((N_RG, 2, RG_BLOCKS, RG_BS, RG_BS), RG_BS ** -0.5),
        "rg_ba": nrm((N_RG, 2, D_RNN), 0.01),
        "rg_wx": nrm((N_RG, 2, RG_BLOCKS, RG_BS, RG_BS), RG_BS ** -0.5),
        "rg_bx": nrm((N_RG, 2, D_RNN), 0.01),
        "rg_lambda": rg_lambda,
        "rg_w_out": nrm((N_RG, D_RNN, D_MODEL), D_RNN ** -0.5),
        "mla_w_dqkv": nrm((N_MLA, D_MODEL, Q_LORA + KV_LORA + QK_ROPE), D_MODEL ** -0.5),
        "mla_norm_q": 1.0 + nrm((N_MLA, Q_LORA), 0.02),
        "mla_norm_kv": 1.0 + nrm((N_MLA, KV_LORA), 0.02),
        "mla_w_uq": nrm((N_MLA, Q_LORA, N_HEADS * (QK_NOPE + QK_ROPE)), Q_LORA ** -0.5),
        "mla_w_ukv": nrm((N_MLA, KV_LORA, N_HEADS * (QK_NOPE + V_DIM)), KV_LORA ** -0.5),
        "mla_w_o": nrm((N_MLA, N_HEADS * V_DIM, D_MODEL), (N_HEADS * V_DIM) ** -0.5),
        "final_norm": 1.0 + nrm((D_MODEL,), 0.02),
    }


def reference(x_prompt, x_sample, state_rglru, cache_ckv, cache_kpe, c, c_ctx,
              ada_w, ada_b, norm_mix, norm_mlp, mlp_w1, mlp_w2,
              rg_w_in, rg_conv_w, rg_conv_b, rg_wa, rg_ba, rg_wx, rg_bx, rg_lambda, rg_w_out,
              mla_w_dqkv, mla_norm_q, mla_norm_kv, mla_w_uq, mla_w_ukv, mla_w_o,
              final_norm):
    xc = x_prompt
    cond_ctx = jnp.broadcast_to(c_ctx, (xc.shape[0], D_MODEL))
    rg_states, ckv_list, kpe_list = [], [], []
    for l in range(DEPTH):
        sh1, sc1, g1, sh2, sc2, g2 = adaln_params(cond_ctx, ada_w[l], ada_b[l])
        h = modulate(rms_norm(xc, norm_mix[l]), sh1, sc1)
        j = l // N_MIXERS
        if l % N_MIXERS == 0:
            h0 = jnp.zeros((xc.shape[0], 2, D_RNN), xc.dtype)
            out, fin = rglru_block(h, h0, rg_w_in[j], rg_conv_w[j], rg_conv_b[j], rg_wa[j], rg_ba[j],
                                   rg_wx[j], rg_bx[j], rg_lambda[j], rg_w_out[j])
            rg_states.append(fin)
        else:
            q_nope, q_pe, ckv, kpe = mla_project(h, mla_w_dqkv[j], mla_norm_q[j], mla_norm_kv[j], mla_w_uq[j])
            k_nope, v = mla_kv(ckv, mla_w_ukv[j])
            out = block_attention(q_nope, q_pe, k_nope, kpe, v) @ mla_w_o[j]
            ckv_list.append(ckv)
            kpe_list.append(kpe)
        xc = xc + g1[:, None, :] * out
        h = modulate(rms_norm(xc, norm_mlp[l]), sh2, sc2)
        xc = xc + g2[:, None, :] * sq_relu_mlp(h, mlp_w1[l], mlp_w2[l])
    y_prompt = rms_norm(xc, final_norm)
    new_state_rglru = jnp.stack(rg_states, axis=1)
    new_cache_ckv = jnp.stack(ckv_list, axis=1)
    new_cache_kpe = jnp.stack(kpe_list, axis=1)

    xs = x_sample
    n_lat = xs.shape[1]
    cos, sin = axial_rope_tables(n_lat)
    for l in range(DEPTH):
        sh1, sc1, g1, sh2, sc2, g2 = adaln_params(c, ada_w[l], ada_b[l])
        h = modulate(rms_norm(xs, norm_mix[l]), sh1, sc1)
        j = l // N_MIXERS
        if l % N_MIXERS == 0:
            out, _ = rglru_block(h, state_rglru[:, j], rg_w_in[j], rg_conv_w[j], rg_conv_b[j], rg_wa[j],
                                 rg_ba[j], rg_wx[j], rg_bx[j], rg_lambda[j], rg_w_out[j])
        else:
            q_nope, q_pe, ckv, kpe = mla_project(h, mla_w_dqkv[j], mla_norm_q[j], mla_norm_kv[j], mla_w_uq[j])
            q_pe = apply_rope(q_pe, cos[:, None, :], sin[:, None, :])
            kpe = apply_rope(kpe, cos, sin)
            k_nope_l, v_l = mla_kv(ckv, mla_w_ukv[j])
            k_nope_c, v_c = mla_kv(cache_ckv[:, j], mla_w_ukv[j])
            k_nope = jnp.concatenate([k_nope_l, k_nope_c], axis=1)
            k_pe = jnp.concatenate([kpe, cache_kpe[:, j]], axis=1)
            v = jnp.concatenate([v_l, v_c], axis=1)
            out = block_attention(q_nope, q_pe, k_nope, k_pe, v) @ mla_w_o[j]
        xs = xs + g1[:, None, :] * out
        h = modulate(rms_norm(xs, norm_mlp[l]), sh2, sc2)
        xs = xs + g2[:, None, :] * sq_relu_mlp(h, mlp_w1[l], mlp_w2[l])
    y_sample = rms_norm(xs, final_norm)

    return (y_prompt, y_sample, new_state_rglru, new_cache_ckv, new_cache_kpe)
```

```python
import functools

import jax
import jax.numpy as jnp
from jax import lax
from jax.experimental import pallas as pl
from jax.experimental.pallas import tpu as pltpu

F32 = jnp.float32
BF16 = jnp.bfloat16

EPS = 1e-6
RG_C = 8.0
RG_BS = 128
N_HEADS = 8
QK_NOPE = 128
QK_ROPE = 64
V_DIM = 128
Q_LORA = 512
KV_LORA = 256
GRID_W = 64
ROPE_THETA = 10000.0
HEAD_SLOT = 256

LANES = 128
SUBLANES = 8
VMEM_LIMIT = 56 * 1024 * 1024

COND_ROWS = 16
SAMPLE_ROW0 = 8


def _dot(a, b):
    return jnp.dot(a, b, preferred_element_type=F32)


def _rms(x, g):
    ms = jnp.mean(x * x, axis=-1, keepdims=True)
    return x * lax.rsqrt(ms + EPS) * g


def _modnorm(x, g, shift, scale):
    return _rms(x, g) * (1.0 + scale) + shift


def _sigmoid(x):
    return 0.5 * jnp.tanh(0.5 * x) + 0.5


def _gelu_tanh(x):
    c = 0.7978845608028654
    return 0.5 * x * (1.0 + jnp.tanh(c * (x + 0.044715 * (x * x * x))))


def _ada_kernel(cond_ref, w_ref, b_ref, o_ref):
    c = cond_ref[...]
    s = c * _sigmoid(c)
    o_ref[0] = jnp.dot(s, w_ref[0], preferred_element_type=F32,
                       precision=lax.Precision.HIGHEST) + b_ref[0]


def _ada_call(cond, ada_w, ada_b):
    depth, d, n6 = ada_w.shape
    tn = 1536
    return pl.pallas_call(
        _ada_kernel,
        grid=(depth, n6 // tn),
        in_specs=[
            pl.BlockSpec((COND_ROWS, d), lambda l, n: (0, 0)),
            pl.BlockSpec((1, d, tn), lambda l, n: (l, 0, n)),
            pl.BlockSpec((1, 1, tn), lambda l, n: (l, 0, n)),
        ],
        out_specs=pl.BlockSpec((1, COND_ROWS, tn), lambda l, n: (l, 0, n)),
        out_shape=jax.ShapeDtypeStruct((depth, COND_ROWS, n6), F32),
        compiler_params=pltpu.CompilerParams(
            dimension_semantics=("arbitrary", "arbitrary"), vmem_limit_bytes=VMEM_LIMIT),
        name="ada",
    )(cond, ada_w, ada_b.reshape(depth, 1, n6))


def _post_kernel(x_ref, y_ref, mod_ref, gn_ref, wp_ref, w1_ref, w2_ref, fn_ref, o_ref, *,
                 ff_chunk, final):
    mod = mod_ref[0, 0]
    g1, sh2, sc2, g2 = mod[2:3], mod[3:4], mod[4:5], mod[5:6]
    x1 = x_ref[...] + g1 * _dot(y_ref[...], wp_ref[...])
    h = _modnorm(x1, gn_ref[...], sh2, sc2).astype(BF16)
    ff = w1_ref.shape[1]
    acc = jnp.zeros_like(x1)
    for c in range(ff // ff_chunk):
        a = jnp.maximum(_dot(h, w1_ref[:, c * ff_chunk:(c + 1) * ff_chunk]), 0.0)
        acc = acc + _dot((a * a).astype(BF16), w2_ref[c * ff_chunk:(c + 1) * ff_chunk, :])
    out = x1 + g2 * acc
    if final:
        out = _rms(out, fn_ref[...])
    o_ref[...] = out


def _post_call(x, y, mod, layer, row_of_tile, gn, wp, w1, w2, fn, *, tm, final):
    ntok, d = x.shape
    ff = w1.shape[1]
    const = dict(pipeline_mode=pl.Buffered(1))
    return pl.pallas_call(
        functools.partial(_post_kernel, ff_chunk=1024, final=final),
        grid=(ntok // tm,),
        in_specs=[
            pl.BlockSpec((tm, d), lambda i: (i, 0)),
            pl.BlockSpec((tm, d), lambda i: (i, 0)),
            pl.BlockSpec((1, 1, 6, d), lambda i: (layer, row_of_tile(i), 0, 0)),
            pl.BlockSpec((1, d), lambda i: (0, 0)),
            pl.BlockSpec((d, d), lambda i: (0, 0), **const),
            pl.BlockSpec((d, ff), lambda i: (0, 0), **const),
            pl.BlockSpec((ff, d), lambda i: (0, 0), **const),
            pl.BlockSpec((1, d), lambda i: (0, 0)),
        ],
        out_specs=pl.BlockSpec((tm, d), lambda i: (i, 0)),
        out_shape=jax.ShapeDtypeStruct((ntok, d), F32),
        compiler_params=pltpu.CompilerParams(
            dimension_semantics=("arbitrary",), vmem_limit_bytes=VMEM_LIMIT),
        name="post",
    )(x, y, mod, gn, wp, w1, w2, fn)


def _rg_kernel(x_ref, mod_ref, gn_ref, wx_ref, wy_ref, cw_ref, cb_ref, wg_ref, bg_ref, lam_ref,
               h0_ref, y_ref, fin_ref,
               h_scr, xb_scr, yb_scr, af_scr, bf_scr, ab_scr, bb_scr, hf_scr, ys_scr, *,
               seq, seg, chunk):
    n_slab = wx_ref.shape[1] // LANES
    n_chunk = seq // chunk
    pad = SUBLANES * seg - seq
    halo = SUBLANES

    @pl.when(pl.program_id(1) == 0)
    def _():
        mod = mod_ref[0, 0]
        for c in range(n_chunk):
            rows = slice(c * chunk, (c + 1) * chunk)
            h_scr[rows, :] = _modnorm(x_ref[0, rows, :], gn_ref[...], mod[0:1], mod[1:2]).astype(BF16)

    xb_scr[0:halo, :] = jnp.zeros((halo, xb_scr.shape[1]), F32)
    xb_scr[halo + seq:2 * halo + seq, :] = jnp.zeros((halo, xb_scr.shape[1]), F32)
    for c in range(n_chunk):
        rows = slice(c * chunk, (c + 1) * chunk)
        hc = h_scr[rows, :]
        xb_scr[halo + c * chunk:halo + (c + 1) * chunk, :] = _dot(hc, wx_ref[...])
        gy = _gelu_tanh(_dot(hc, wy_ref[...]))
        for s in range(n_slab):
            yb_scr[s, rows, :] = gy[:, s * LANES:(s + 1) * LANES]

    for s in range(n_slab):
        tail = slice(seq, seq + pad)
        ones = jnp.ones((pad, LANES), F32)
        zeros = jnp.zeros((pad, LANES), F32)
        af_scr[s, tail, :] = ones
        ab_scr[s, tail, :] = ones
        bf_scr[s, tail, :] = zeros
        bb_scr[s, tail, :] = zeros
        yb_scr[s, tail, :] = zeros

    cw = cw_ref[...]
    for c in range(n_chunk):
        base = halo + c * chunk
        rows = slice(c * chunk, (c + 1) * chunk)
        xc = cb_ref[...] + cw[0:1] * xb_scr[base - 1:base - 1 + chunk, :]
        for k in range(1, 4):
            xc = xc + cw[k:k + 1] * xb_scr[base - 1 + k:base - 1 + k + chunk, :]
        for s in range(n_slab):
            lanes = slice(s * LANES, (s + 1) * LANES)
            xs = xc[:, lanes]
            g = _dot(xs.astype(BF16), wg_ref[s]) + bg_ref[s]
            for d, (a_scr, b_scr) in enumerate(((af_scr, bf_scr), (ab_scr, bb_scr))):
                r = _sigmoid(g[:, d * LANES:(d + 1) * LANES])
                i = _sigmoid(g[:, (2 + d) * LANES:(3 + d) * LANES])
                lam = lam_ref[d:d + 1, lanes]
                sp = jnp.maximum(-lam, 0.0) + jnp.log1p(jnp.exp(-jnp.abs(lam)))
                log_a = (-RG_C) * r * sp
                a = jnp.exp(log_a)
                a_scr[s, rows, :] = a
                b_scr[s, rows, :] = jnp.sqrt(1.0 - a * a) * (i * xs)

    def seg_rows(j):
        return pl.ds(j, SUBLANES, stride=seg)

    def pass1(j, carry):
        jb = seg - 1 - j
        out = []
        for s in range(n_slab):
            pf, hf, pb, hb = carry[s]
            a = af_scr[s, seg_rows(j), :]
            b = bf_scr[s, seg_rows(j), :]
            a2 = ab_scr[s, seg_rows(jb), :]
            b2 = bb_scr[s, seg_rows(jb), :]
            out.append((pf * a, a * hf + b, pb * a2, a2 * hb + b2))
        return tuple(out)

    one = jnp.ones((SUBLANES, LANES), F32)
    zero = jnp.zeros((SUBLANES, LANES), F32)
    totals = lax.fori_loop(0, seg, pass1, tuple((one, zero, one, zero) for _ in range(n_slab)))

    rid = lax.broadcasted_iota(jnp.int32, (SUBLANES, LANES), 0)
    init_f, init_b = [], []
    for s in range(n_slab):
        lanes = slice(s * LANES, (s + 1) * LANES)
        pf, hf, pb, hb = totals[s]
        st = h0_ref[0, 0:1, lanes]
        ini = zero
        for k in range(SUBLANES):
            ini = jnp.where(rid == k, st, ini)
            st = hf[k:k + 1] + pf[k:k + 1] * st
        init_f.append(ini)
        fin_ref[0, 0:1, lanes] = st
        st = h0_ref[0, 1:2, lanes]
        ini = zero
        for k in range(SUBLANES - 1, -1, -1):
            ini = jnp.where(rid == k, st, ini)
            st = hb[k:k + 1] + pb[k:k + 1] * st
        init_b.append(ini)
        fin_ref[0, 1:2, lanes] = st

    def pass2f(j, hs):
        out = []
        for s in range(n_slab):
            h = af_scr[s, seg_rows(j), :] * hs[s] + bf_scr[s, seg_rows(j), :]
            hf_scr[s, seg_rows(j), :] = h
            out.append(h)
        return tuple(out)

    lax.fori_loop(0, seg, pass2f, tuple(init_f))

    def pass2b(j, hs):
        jb = seg - 1 - j
        out = []
        for s in range(n_slab):
            h = ab_scr[s, seg_rows(jb), :] * hs[s] + bb_scr[s, seg_rows(jb), :]
            ys_scr[s, seg_rows(jb), :] = (hf_scr[s, seg_rows(jb), :] + h) * yb_scr[s, seg_rows(jb), :]
            out.append(h)
        return tuple(out)

    lax.fori_loop(0, seg, pass2b, tuple(init_b))

    for s in range(n_slab):
        y_ref[0, :, s * LANES:(s + 1) * LANES] = ys_scr[s, 0:seq, :].astype(BF16)


def _odd_segment_length(seq):
    seg = -(-seq // SUBLANES)
    return seg if seg % 2 == 1 else seg + 1


def _rg_call(x, mod, layer, row_of_batch, gn, w_in, conv_w, conv_b, wg, bg, lam, h0, *, cblk):
    b, seq, d = x.shape
    d_rnn = lam.shape[1]
    ncb = d_rnn // cblk
    n_slab = cblk // LANES
    seg = _odd_segment_length(seq)
    seq_pad = SUBLANES * seg
    chunk = min(seq, 512)
    slab = pltpu.VMEM((n_slab, seq_pad, LANES), F32)
    return pl.pallas_call(
        functools.partial(_rg_kernel, seq=seq, seg=seg, chunk=chunk),
        grid=(b, ncb),
        in_specs=[
            pl.BlockSpec((1, seq, d), lambda i, c: (i, 0, 0)),
            pl.BlockSpec((1, 1, 6, d), lambda i, c: (layer, row_of_batch(i), 0, 0)),
            pl.BlockSpec((1, d), lambda i, c: (0, 0)),
            pl.BlockSpec((d, cblk), lambda i, c: (0, c)),
            pl.BlockSpec((d, cblk), lambda i, c: (0, ncb + c)),
            pl.BlockSpec((4, cblk), lambda i, c: (0, c)),
            pl.BlockSpec((1, cblk), lambda i, c: (0, c)),
            pl.BlockSpec((n_slab, RG_BS, 4 * RG_BS), lambda i, c: (c, 0, 0)),
            pl.BlockSpec((n_slab, 1, 4 * RG_BS), lambda i, c: (c, 0, 0)),
            pl.BlockSpec((2, cblk), lambda i, c: (0, c)),
            pl.BlockSpec((1, 2, cblk), lambda i, c: (i, 0, c)),
        ],
        out_specs=[
            pl.BlockSpec((1, seq, cblk), lambda i, c: (i, 0, c)),
            pl.BlockSpec((1, 2, cblk), lambda i, c: (i, 0, c)),
        ],
        out_shape=[
            jax.ShapeDtypeStruct((b, seq, d_rnn), BF16),
            jax.ShapeDtypeStruct((b, 2, d_rnn), F32),
        ],
        scratch_shapes=[
            pltpu.VMEM((seq, d), BF16),
            pltpu.VMEM((seq + 2 * SUBLANES, cblk), F32),
            slab, slab, slab, slab, slab, slab, slab,
        ],
        compiler_params=pltpu.CompilerParams(
            dimension_semantics=("arbitrary", "arbitrary"), vmem_limit_bytes=VMEM_LIMIT),
        name="rg_core",
    )(x, mod, gn, w_in, w_in, conv_w, conv_b, wg, bg, lam, h0)


def _rope_pairs(v, tab):
    p = v * tab
    return p + pltpu.roll(p, QK_ROPE, axis=1)


def _mlaproj_kernel(x_ref, mod_ref, gn_ref, wd_ref, gq_ref, gkv_ref, wuq_ref, wk_ref, wv_ref, tab_ref,
                    q_ref, k_ref, v_ref, *cache_refs, q_scale):
    mod = mod_ref[0, 0]
    h = _modnorm(x_ref[...], gn_ref[...], mod[0:1], mod[1:2]).astype(BF16)
    proj = _dot(h, wd_ref[...])
    cq = _rms(proj[:, :Q_LORA], gq_ref[...])
    ckv = _rms(proj[:, Q_LORA:Q_LORA + KV_LORA], gkv_ref[...])
    kp2 = proj[:, Q_LORA + KV_LORA:]
    if cache_refs:
        ckv_ref, kpe_ref = cache_refs
        ckv_ref[...] = ckv
        kpe_ref[...] = kp2[:, :QK_ROPE]
    tab = tab_ref[...]
    lane = lax.broadcasted_iota(jnp.int32, kp2.shape, 1)
    kr = jnp.where(lane < QK_ROPE, _rope_pairs(kp2, tab), 0.0).astype(BF16)
    q = _dot(cq.astype(BF16), wuq_ref[...])
    ckvb = ckv.astype(BF16)
    kn = _dot(ckvb, wk_ref[...])
    v_ref[...] = _dot(ckvb, wv_ref[...]).astype(BF16)
    for hd in range(N_HEADS):
        lo = hd * HEAD_SLOT
        q_ref[:, lo:lo + QK_NOPE] = (q[:, lo:lo + QK_NOPE] * q_scale).astype(BF16)
        qr = _rope_pairs(q[:, lo + QK_NOPE:lo + HEAD_SLOT], tab) * q_scale
        q_ref[:, lo + QK_NOPE:lo + HEAD_SLOT] = qr.astype(BF16)
        k_ref[:, lo:lo + QK_NOPE] = kn[:, hd * QK_NOPE:(hd + 1) * QK_NOPE].astype(BF16)
        k_ref[:, lo + QK_NOPE:lo + HEAD_SLOT] = kr


def _mlaproj_call(x, mod, layer, row_of_tile, gn, wd, gq, gkv, wuq, wk, wv, tab, tab_tile, *,
                  tm, emit_cache):
    ntok, d = x.shape
    const = dict(pipeline_mode=pl.Buffered(1))
    out_specs = [
        pl.BlockSpec((tm, N_HEADS * HEAD_SLOT), lambda i: (i, 0)),
        pl.BlockSpec((tm, N_HEADS * HEAD_SLOT), lambda i: (i, 0)),
        pl.BlockSpec((tm, N_HEADS * V_DIM), lambda i: (i, 0)),
    ]
    out_shape = [
        jax.ShapeDtypeStruct((ntok, N_HEADS * HEAD_SLOT), BF16),
        jax.ShapeDtypeStruct((ntok, N_HEADS * HEAD_SLOT), BF16),
        jax.ShapeDtypeStruct((ntok, N_HEADS * V_DIM), BF16),
    ]
    if emit_cache:
        out_specs += [pl.BlockSpec((tm, KV_LORA), lambda i: (i, 0)),
                      pl.BlockSpec((tm, QK_ROPE), lambda i: (i, 0))]
        out_shape += [jax.ShapeDtypeStruct((ntok, KV_LORA), F32),
                      jax.ShapeDtypeStruct((ntok, QK_ROPE), F32)]
    return pl.pallas_call(
        functools.partial(_mlaproj_kernel, q_scale=float((QK_NOPE + QK_ROPE) ** -0.5)),
        grid=(ntok // tm,),
        in_specs=[
            pl.BlockSpec((tm, d), lambda i: (i, 0)),
            pl.BlockSpec((1, 1, 6, d), lambda i: (layer, row_of_tile(i), 0, 0)),
            pl.BlockSpec((1, d), lambda i: (0, 0)),
            pl.BlockSpec(wd.shape, lambda i: (0, 0), **const),
            pl.BlockSpec((1, Q_LORA), lambda i: (0, 0)),
            pl.BlockSpec((1, KV_LORA), lambda i: (0, 0)),
            pl.BlockSpec(wuq.shape, lambda i: (0, 0), **const),
            pl.BlockSpec(wk.shape, lambda i: (0, 0), **const),
            pl.BlockSpec(wv.shape, lambda i: (0, 0), **const),
            pl.BlockSpec((tm, LANES), lambda i: (tab_tile(i), 0)),
        ],
        out_specs=out_specs,
        out_shape=out_shape,
        compiler_params=pltpu.CompilerParams(
            dimension_semantics=("arbitrary",), vmem_limit_bytes=VMEM_LIMIT),
        name="mla_proj",
    )(x, mod, gn, wd, gq, gkv, wuq, wk, wv, tab)


def _cachekv_kernel(ckv_ref, kpe_ref, wk_ref, wv_ref, k_ref, v_ref):
    ckvb = ckv_ref[...].astype(BF16)
    kn = _dot(ckvb, wk_ref[...])
    v_ref[...] = _dot(ckvb, wv_ref[...]).astype(BF16)
    kp = kpe_ref[...].astype(BF16)
    for hd in range(N_HEADS):
        lo = hd * HEAD_SLOT
        k_ref[:, lo:lo + QK_NOPE] = kn[:, hd * QK_NOPE:(hd + 1) * QK_NOPE].astype(BF16)
        k_ref[:, lo + QK_NOPE:lo + HEAD_SLOT] = kp


def _cachekv_call(ckv, kpe_padded, wk, wv, *, tm):
    ntok = ckv.shape[0]
    return pl.pallas_call(
        _cachekv_kernel,
        grid=(ntok // tm,),
        in_specs=[
            pl.BlockSpec((tm, KV_LORA), lambda i: (i, 0)),
            pl.BlockSpec((tm, LANES), lambda i: (i, 0)),
            pl.BlockSpec(wk.shape, lambda i: (0, 0)),
            pl.BlockSpec(wv.shape, lambda i: (0, 0)),
        ],
        out_specs=[
            pl.BlockSpec((tm, N_HEADS * HEAD_SLOT), lambda i: (i, 0)),
            pl.BlockSpec((tm, N_HEADS * V_DIM), lambda i: (i, 0)),
        ],
        out_shape=[
            jax.ShapeDtypeStruct((ntok, N_HEADS * HEAD_SLOT), BF16),
            jax.ShapeDtypeStruct((ntok, N_HEADS * V_DIM), BF16),
        ],
        compiler_params=pltpu.CompilerParams(
            dimension_semantics=("arbitrary",), vmem_limit_bytes=VMEM_LIMIT),
        name="mla_cache_kv",
    )(ckv, kpe_padded, wk, wv)


def _qk(q, k):
    return lax.dot_general(q, k, (((1,), (1,)), ((), ())), preferred_element_type=F32)


def _attn_kernel(q_ref, k_ref, v_ref, *rest):
    if len(rest) == 3:
        kc_ref, vc_ref, o_ref = rest
    else:
        kc_ref = vc_ref = None
        (o_ref,) = rest
    for hd in range(N_HEADS):
        slot = slice(hd * HEAD_SLOT, (hd + 1) * HEAD_SLOT)
        vcol = slice(hd * V_DIM, (hd + 1) * V_DIM)
        qh = q_ref[:, slot]
        s = _qk(qh, k_ref[0, :, slot])
        m = jnp.max(s, axis=-1, keepdims=True)
        if kc_ref is not None:
            sc = _qk(qh, kc_ref[0, :, slot])
            m = jnp.maximum(m, jnp.max(sc, axis=-1, keepdims=True))
        p = jnp.exp(s - m)
        l = jnp.sum(p, axis=-1, keepdims=True)
        pv = _dot(p.astype(BF16), v_ref[0, :, vcol])
        if kc_ref is not None:
            pc = jnp.exp(sc - m)
            l = l + jnp.sum(pc, axis=-1, keepdims=True)
            pv = pv + _dot(pc.astype(BF16), vc_ref[0, :, vcol])
        o_ref[:, vcol] = (pv * (1.0 / l)).astype(BF16)


def _attn_call(q, k, v, kc, vc, *, batch, tq):
    ntok = q.shape[0]
    seq = ntok // batch
    nq = seq // tq
    kw = N_HEADS * HEAD_SLOT
    vw = N_HEADS * V_DIM
    args = [q, k.reshape(batch, seq, kw), v.reshape(batch, seq, vw)]
    in_specs = [
        pl.BlockSpec((tq, kw), lambda b, i: (b * nq + i, 0)),
        pl.BlockSpec((1, seq, kw), lambda b, i: (b, 0, 0)),
        pl.BlockSpec((1, seq, vw), lambda b, i: (b, 0, 0)),
    ]
    if kc is not None:
        past = kc.shape[0] // batch
        args += [kc.reshape(batch, past, kw), vc.reshape(batch, past, vw)]
        in_specs += [pl.BlockSpec((1, past, kw), lambda b, i: (b, 0, 0)),
                     pl.BlockSpec((1, past, vw), lambda b, i: (b, 0, 0))]
    return pl.pallas_call(
        _attn_kernel,
        grid=(batch, nq),
        in_specs=in_specs,
        out_specs=pl.BlockSpec((tq, vw), lambda b, i: (b * nq + i, 0)),
        out_shape=jax.ShapeDtypeStruct((ntok, vw), BF16),
        compiler_params=pltpu.CompilerParams(
            dimension_semantics=("arbitrary", "arbitrary"), vmem_limit_bytes=VMEM_LIMIT),
        name="mla_attn",
    )(*args)


def _pair_partner(w):
    k, n = w.shape
    wp = w.reshape(k, n // 2, 2)
    return jnp.stack([-wp[..., 1], wp[..., 0]], axis=-1).reshape(k, n)


def _rope_table(n):
    rows = n // GRID_W
    row = jnp.repeat(jnp.arange(rows, dtype=F32), GRID_W)
    col = jnp.tile(jnp.arange(GRID_W, dtype=F32), rows)
    half = QK_ROPE // 2
    inv = ROPE_THETA ** (-jnp.arange(0, half, 2, dtype=F32) / half)
    ang = jnp.concatenate([row[:, None] * inv, col[:, None] * inv], axis=-1)
    ang = jnp.repeat(ang, 2, axis=-1)
    return jnp.concatenate([jnp.cos(ang), jnp.sin(ang)], axis=-1)


def _identity_rope_table(n):
    return jnp.concatenate([jnp.ones((n, QK_ROPE), F32), jnp.zeros((n, QK_ROPE), F32)], axis=-1)


def kernel(x_prompt, x_sample, state_rglru, cache_ckv, cache_kpe, c, c_ctx, ada_w, ada_b, norm_mix, norm_mlp, mlp_w1, mlp_w2, rg_w_in, rg_conv_w, rg_conv_b, rg_wa, rg_ba, rg_wx, rg_bx, rg_lambda, rg_w_out, mla_w_dqkv, mla_norm_q, mla_norm_kv, mla_w_uq, mla_w_ukv, mla_w_o, final_norm):
    batch, seq, d = x_prompt.shape
    dec_batch, dec_seq, _ = x_sample.shape
    depth = ada_w.shape[0]
    d_rnn = rg_lambda.shape[-1]
    past = cache_ckv.shape[2]
    assert dec_batch == COND_ROWS - SAMPLE_ROW0

    cond = jnp.concatenate(
        [jnp.broadcast_to(c_ctx[None, :], (SAMPLE_ROW0, d)), c], axis=0)
    mod = _ada_call(cond, ada_w, ada_b).reshape(depth, COND_ROWS, 6, d)

    w1 = mlp_w1.astype(BF16)
    w2 = mlp_w2.astype(BF16)
    w_in = rg_w_in.astype(BF16)
    w_out = rg_w_out.astype(BF16)
    w_o = mla_w_o.astype(BF16)
    n_rg, n_mla = rg_w_in.shape[0], mla_w_dqkv.shape[0]
    nblk = d_rnn // RG_BS
    wg = jnp.concatenate([rg_wa[:, 0], rg_wa[:, 1], rg_wx[:, 0], rg_wx[:, 1]], axis=-1).astype(BF16)
    bg = jnp.concatenate([rg_ba[:, 0].reshape(n_rg, nblk, 1, RG_BS), rg_ba[:, 1].reshape(n_rg, nblk, 1, RG_BS),
                          rg_bx[:, 0].reshape(n_rg, nblk, 1, RG_BS), rg_bx[:, 1].reshape(n_rg, nblk, 1, RG_BS)],
                         axis=-1)
    lo = Q_LORA + KV_LORA
    wd, wuq, wk, wv = [], [], [], []
    for j in range(n_mla):
        w = mla_w_dqkv[j]
        wd.append(jnp.concatenate([w, _pair_partner(w[:, lo:])], axis=1).astype(BF16))
        wq = mla_w_uq[j].reshape(Q_LORA, N_HEADS, QK_NOPE + QK_ROPE)
        wq_pe = wq[..., QK_NOPE:]
        wq_pp = jnp.stack([_pair_partner(wq_pe[:, hd]) for hd in range(N_HEADS)], axis=1)
        wuq.append(jnp.concatenate([wq, wq_pp], axis=-1).reshape(Q_LORA, N_HEADS * HEAD_SLOT).astype(BF16))
        wkv = mla_w_ukv[j].reshape(KV_LORA, N_HEADS, QK_NOPE + V_DIM)
        wk.append(wkv[..., :QK_NOPE].reshape(KV_LORA, N_HEADS * QK_NOPE).astype(BF16))
        wv.append(wkv[..., QK_NOPE:].reshape(KV_LORA, N_HEADS * V_DIM).astype(BF16))

    fn = final_norm.reshape(1, d)
    tab_sample = _rope_table(dec_seq)
    tm_ctx = 512
    tm_dec = 512
    tab_ctx = _identity_rope_table(tm_ctx)

    def run_stream(x, h0_all, cache, *, sample):
        b, t, _ = x.shape
        tm = tm_dec if sample else tm_ctx
        if sample:
            row_of_batch = lambda i: SAMPLE_ROW0 + i
            row_of_tile = lambda i: SAMPLE_ROW0 + i // (t // tm)
            tab, tab_tile = tab_sample, (lambda i: i % (t // tm))
        else:
            row_of_batch = lambda i: 0
            row_of_tile = lambda i: 0
            tab, tab_tile = tab_ctx, (lambda i: 0)
        xt = x.reshape(b * t, d)
        states, ckvs, kpes = [], [], []
        for l in range(depth):
            j = l // 2
            if l % 2 == 0:
                y, fin = _rg_call(xt.reshape(b, t, d), mod, l, row_of_batch, norm_mix[l].reshape(1, d),
                                  w_in[j], rg_conv_w[j], rg_conv_b[j].reshape(1, d_rnn), wg[j], bg[j],
                                  rg_lambda[j], h0_all[:, j], cblk=256)
                states.append(fin)
                y = y.reshape(b * t, d_rnn)
                wp = w_out[j]
            else:
                outs = _mlaproj_call(xt, mod, l, row_of_tile, norm_mix[l].reshape(1, d), wd[j],
                                     mla_norm_q[j].reshape(1, Q_LORA), mla_norm_kv[j].reshape(1, KV_LORA),
                                     wuq[j], wk[j], wv[j], tab, tab_tile, tm=tm, emit_cache=not sample)
                q, k, v = outs[:3]
                if sample:
                    ckv_c, kpe_c = cache
                    kpe_p = jnp.pad(kpe_c[:, j].reshape(b * past, QK_ROPE), ((0, 0), (0, LANES - QK_ROPE)))
                    kc, vc = _cachekv_call(ckv_c[:, j].reshape(b * past, KV_LORA), kpe_p, wk[j], wv[j], tm=512)
                else:
                    kc = vc = None
                    ckvs.append(outs[3].reshape(b, t, KV_LORA))
                    kpes.append(outs[4].reshape(b, t, QK_ROPE))
                y = _attn_call(q, k, v, kc, vc, batch=b, tq=256)
                wp = w_o[j]
            xt = _post_call(xt, y, mod, l, row_of_tile, norm_mlp[l].reshape(1, d), wp, w1[l], w2[l], fn,
                            tm=tm, final=(l == depth - 1))
        return xt.reshape(b, t, d), states, ckvs, kpes

    zeros_h0 = jnp.zeros((batch, n_rg, 2, d_rnn), F32)
    y_prompt, states, ckvs, kpes = run_stream(x_prompt, zeros_h0, None, sample=False)
    y_sample, _, _, _ = run_stream(x_sample, state_rglru, (cache_ckv, cache_kpe), sample=True)
    return (y_prompt, y_sample, jnp.stack(states, axis=1), jnp.stack(ckvs, axis=1), jnp.stack(kpes, axis=1))
```

```python
import functools

import jax
import jax.numpy as jnp
from jax import lax
from jax.experimental import pallas as pl
from jax.experimental.pallas import tpu as pltpu

F32 = jnp.float32
BF16 = jnp.bfloat16

EPS = 1e-6
LOG2_E = 1.4426950408889634
RG_C = 8.0
RG_BS = 128
N_HEADS = 8
QK_NOPE = 128
QK_ROPE = 64
V_DIM = 128
Q_LORA = 512
KV_LORA = 256
GRID_W = 64
ROPE_THETA = 10000.0
HEAD_SLOT = 256

LANES = 128
SUBLANES = 8
VMEM_LIMIT = 56 * 1024 * 1024

COND_ROWS = 16
SAMPLE_ROW0 = 8


def _dot(a, b):
    return jnp.dot(a, b, preferred_element_type=F32)


def _rms(x, g):
    ms = jnp.mean(x * x, axis=-1, keepdims=True)
    return x * lax.rsqrt(ms + EPS) * g


def _modnorm(x, g, shift, scale):
    ms = jnp.mean(x * x, axis=-1, keepdims=True)
    return x * lax.rsqrt(ms + EPS) * (g * (1.0 + scale)) + shift


def _sigmoid(x):
    return 0.5 * jnp.tanh(0.5 * x) + 0.5


def _gelu_tanh(x):
    c = 0.7978845608028654
    xh = 0.5 * x
    return xh + xh * jnp.tanh(x * (c + (c * 0.044715) * (x * x)))


def _ada_kernel(cond_ref, w_ref, b_ref, o_ref):
    c = cond_ref[...]
    s = c * _sigmoid(c)
    o_ref[0] = jnp.dot(s, w_ref[0], preferred_element_type=F32,
                       precision=lax.Precision.HIGHEST) + b_ref[0]


def _ada_call(cond, ada_w, ada_b):
    depth, d, n6 = ada_w.shape
    tn = 1536
    return pl.pallas_call(
        _ada_kernel,
        grid=(depth, n6 // tn),
        in_specs=[
            pl.BlockSpec((COND_ROWS, d), lambda l, n: (0, 0)),
            pl.BlockSpec((1, d, tn), lambda l, n: (l, 0, n)),
            pl.BlockSpec((1, 1, tn), lambda l, n: (l, 0, n)),
        ],
        out_specs=pl.BlockSpec((1, COND_ROWS, tn), lambda l, n: (l, 0, n)),
        out_shape=jax.ShapeDtypeStruct((depth, COND_ROWS, n6), F32),
        compiler_params=pltpu.CompilerParams(
            dimension_semantics=("arbitrary", "arbitrary"), vmem_limit_bytes=VMEM_LIMIT),
        name="ada",
    )(cond, ada_w, ada_b.reshape(depth, 1, n6))


def _post_kernel(x_ref, y_ref, mod_ref, gn_ref, wp_ref, w1_ref, w2_ref, fn_ref, o_ref, *,
                 ff_chunk, final):
    mod = mod_ref[0, 0]
    g1, sh2, sc2, g2 = mod[2:3], mod[3:4], mod[4:5], mod[5:6]
    x1 = x_ref[...] + g1 * _dot(y_ref[...], wp_ref[...])
    h = _modnorm(x1, gn_ref[...], sh2, sc2).astype(BF16)
    ff = w1_ref.shape[1]
    acc = jnp.zeros_like(x1)
    for c in range(ff // ff_chunk):
        a = jnp.maximum(_dot(h, w1_ref[:, c * ff_chunk:(c + 1) * ff_chunk]), 0.0)
        acc = acc + _dot((a * a).astype(BF16), w2_ref[c * ff_chunk:(c + 1) * ff_chunk, :])
    out = x1 + g2 * acc
    if final:
        out = _rms(out, fn_ref[...])
    o_ref[...] = out


def _post_call(x, y, mod, layer, row_of_tile, gn, wp, w1, w2, fn, *, tm, final):
    ntok, d = x.shape
    ff = w1.shape[1]
    const = dict(pipeline_mode=pl.Buffered(1))
    return pl.pallas_call(
        functools.partial(_post_kernel, ff_chunk=1024, final=final),
        grid=(ntok // tm,),
        in_specs=[
            pl.BlockSpec((tm, d), lambda i: (i, 0)),
            pl.BlockSpec((tm, d), lambda i: (i, 0)),
            pl.BlockSpec((1, 1, 6, d), lambda i: (layer, row_of_tile(i), 0, 0)),
            pl.BlockSpec((1, d), lambda i: (0, 0)),
            pl.BlockSpec((d, d), lambda i: (0, 0), **const),
            pl.BlockSpec((d, ff), lambda i: (0, 0), **const),
            pl.BlockSpec((ff, d), lambda i: (0, 0), **const),
            pl.BlockSpec((1, d), lambda i: (0, 0)),
        ],
        out_specs=pl.BlockSpec((tm, d), lambda i: (i, 0)),
        out_shape=jax.ShapeDtypeStruct((ntok, d), F32),
        compiler_params=pltpu.CompilerParams(
            dimension_semantics=("arbitrary",), vmem_limit_bytes=VMEM_LIMIT),
        name="post",
    )(x, y, mod, gn, wp, w1, w2, fn)


def _rg_kernel(x_ref, mod_ref, gn_ref, wx_ref, wy_ref, cw_ref, cb_ref, wg_ref, bg_ref, lam_ref,
               h0_ref, y_ref, fin_ref,
               h_scr, xb_scr, yb_scr, af_scr, bf_scr, ab_scr, bb_scr, *,
               seq, seg, chunk):
    n_slab = wx_ref.shape[1] // LANES
    n_chunk = seq // chunk
    pad = SUBLANES * seg - seq
    halo = SUBLANES

    @pl.when(pl.program_id(1) == 0)
    def _():
        mod = mod_ref[0, 0]
        for c in range(n_chunk):
            rows = slice(c * chunk, (c + 1) * chunk)
            h_scr[rows, :] = _modnorm(x_ref[0, rows, :], gn_ref[...], mod[0:1], mod[1:2]).astype(BF16)

    def xb_rows(t0, n):
        return pl.ds(2 * (halo + t0), n, stride=2)

    for s in range(n_slab):
        xb_scr[s, 0:2 * halo, :] = jnp.zeros((2 * halo, LANES), F32)
        xb_scr[s, 2 * (halo + seq):2 * (2 * halo + seq), :] = jnp.zeros((2 * halo, LANES), F32)
    for c in range(n_chunk):
        rows = slice(c * chunk, (c + 1) * chunk)
        hc = h_scr[rows, :]
        ux = _dot(hc, wx_ref[...])
        for s in range(n_slab):
            xb_scr[s, xb_rows(c * chunk, chunk), :] = ux[:, s * LANES:(s + 1) * LANES]
        yb_scr[rows, :] = _gelu_tanh(_dot(hc, wy_ref[...]))

    for s in range(n_slab):
        tail = slice(seq, seq + pad)
        ones = jnp.ones((pad, LANES), F32)
        zeros = jnp.zeros((pad, LANES), F32)
        af_scr[s, tail, :] = ones
        ab_scr[s, tail, :] = ones
        bf_scr[s, tail, :] = zeros
        bb_scr[s, tail, :] = zeros

    cw = cw_ref[...]
    for s in range(n_slab):
        lanes = slice(s * LANES, (s + 1) * LANES)
        c1 = []
        for d in range(2):
            lam = lam_ref[d:d + 1, lanes]
            sp = jnp.maximum(-lam, 0.0) + jnp.log1p(jnp.exp(-jnp.abs(lam)))
            c1.append((-0.5 * RG_C * LOG2_E) * sp)
        for c in range(n_chunk):
            rows = slice(c * chunk, (c + 1) * chunk)
            xs = cb_ref[:, lanes] + cw[0:1, lanes] * xb_scr[s, xb_rows(c * chunk - 1, chunk), :]
            for k in range(1, 4):
                xs = xs + cw[k:k + 1, lanes] * xb_scr[s, xb_rows(c * chunk - 1 + k, chunk), :]
            g = _dot(xs.astype(BF16), wg_ref[s]) + bg_ref[s]
            xh = 0.5 * xs
            for d, (a_scr, b_scr) in enumerate(((af_scr, bf_scr), (ab_scr, bb_scr))):
                tr = jnp.tanh(g[:, d * LANES:(d + 1) * LANES])
                ti = jnp.tanh(g[:, (2 + d) * LANES:(3 + d) * LANES])
                a = jnp.exp2(c1[d] * tr + c1[d])
                a_scr[s, rows, :] = a
                v = 1.0 - a * a
                root = jnp.where(v > 0.0, v * lax.rsqrt(v), 0.0)
                b_scr[s, rows, :] = root * (ti * xh + xh)

    def seg_rows(j):
        return pl.ds(j, SUBLANES, stride=seg)

    def run_steps(step, init):
        unroll = 4
        main = seg // unroll

        def body(i, carry):
            for u in range(unroll):
                carry = step(i * unroll + u, carry)
            return carry

        carry = lax.fori_loop(0, main, body, init)
        for j in range(main * unroll, seg):
            carry = step(j, carry)
        return carry

    def pass1(j, carry):
        jb = seg - 1 - j
        out = []
        for s in range(n_slab):
            pf, hf, pb, hb = carry[s]
            a = af_scr[s, seg_rows(j), :]
            b = bf_scr[s, seg_rows(j), :]
            a2 = ab_scr[s, seg_rows(jb), :]
            b2 = bb_scr[s, seg_rows(jb), :]
            out.append((pf * a, a * hf + b, pb * a2, a2 * hb + b2))
        return tuple(out)

    one = jnp.ones((SUBLANES, LANES), F32)
    zero = jnp.zeros((SUBLANES, LANES), F32)
    totals = run_steps(pass1, tuple((one, zero, one, zero) for _ in range(n_slab)))

    rid = lax.broadcasted_iota(jnp.int32, (SUBLANES, LANES), 0)
    init_f, init_b = [], []
    for s in range(n_slab):
        lanes = slice(s * LANES, (s + 1) * LANES)
        pf, hf, pb, hb = totals[s]
        st = h0_ref[0, 0:1, lanes]
        ini = zero
        for k in range(SUBLANES):
            ini = jnp.where(rid == k, st, ini)
            st = hf[k:k + 1] + pf[k:k + 1] * st
        init_f.append(ini)
        fin_ref[0, 0:1, lanes] = st
        st = h0_ref[0, 1:2, lanes]
        ini = zero
        for k in range(SUBLANES - 1, -1, -1):
            ini = jnp.where(rid == k, st, ini)
            st = hb[k:k + 1] + pb[k:k + 1] * st
        init_b.append(ini)
        fin_ref[0, 1:2, lanes] = st

    def pass2(j, carry):
        jb = seg - 1 - j
        out = []
        for s in range(n_slab):
            hf, hb = carry[s]
            hf = af_scr[s, seg_rows(j), :] * hf + bf_scr[s, seg_rows(j), :]
            bf_scr[s, seg_rows(j), :] = hf
            hb = ab_scr[s, seg_rows(jb), :] * hb + bb_scr[s, seg_rows(jb), :]
            bb_scr[s, seg_rows(jb), :] = hb
            out.append((hf, hb))
        return tuple(out)

    run_steps(pass2, tuple(zip(init_f, init_b)))

    for c in range(n_chunk):
        rows = slice(c * chunk, (c + 1) * chunk)
        for s in range(n_slab):
            lanes = slice(s * LANES, (s + 1) * LANES)
            y_ref[0, rows, lanes] = ((bf_scr[s, rows, :] + bb_scr[s, rows, :]) * yb_scr[rows, lanes]).astype(BF16)


def _odd_segment_length(seq):
    seg = -(-seq // SUBLANES)
    return seg if seg % 2 == 1 else seg + 1


def _rg_call(x, mod, layer, row_of_batch, gn, w_in, conv_w, conv_b, wg, bg, lam, h0, *, cblk):
    b, seq, d = x.shape
    d_rnn = lam.shape[1]
    ncb = d_rnn // cblk
    n_slab = cblk // LANES
    seg = _odd_segment_length(seq)
    seq_pad = SUBLANES * seg
    chunk = min(seq, 512)
    slab = pltpu.VMEM((n_slab, seq_pad, LANES), F32)
    return pl.pallas_call(
        functools.partial(_rg_kernel, seq=seq, seg=seg, chunk=chunk),
        grid=(b, ncb),
        in_specs=[
            pl.BlockSpec((1, seq, d), lambda i, c: (i, 0, 0)),
            pl.BlockSpec((1, 1, 6, d), lambda i, c: (layer, row_of_batch(i), 0, 0)),
            pl.BlockSpec((1, d), lambda i, c: (0, 0)),
            pl.BlockSpec((d, cblk), lambda i, c: (0, c)),
            pl.BlockSpec((d, cblk), lambda i, c: (0, ncb + c)),
            pl.BlockSpec((4, cblk), lambda i, c: (0, c)),
            pl.BlockSpec((1, cblk), lambda i, c: (0, c)),
            pl.BlockSpec((n_slab, RG_BS, 4 * RG_BS), lambda i, c: (c, 0, 0)),
            pl.BlockSpec((n_slab, 1, 4 * RG_BS), lambda i, c: (c, 0, 0)),
            pl.BlockSpec((2, cblk), lambda i, c: (0, c)),
            pl.BlockSpec((1, 2, cblk), lambda i, c: (i, 0, c)),
        ],
        out_specs=[
            pl.BlockSpec((1, seq, cblk), lambda i, c: (i, 0, c)),
            pl.BlockSpec((1, 2, cblk), lambda i, c: (i, 0, c)),
        ],
        out_shape=[
            jax.ShapeDtypeStruct((b, seq, d_rnn), BF16),
            jax.ShapeDtypeStruct((b, 2, d_rnn), F32),
        ],
        scratch_shapes=[
            pltpu.VMEM((seq, d), BF16),
            pltpu.VMEM((n_slab, 2 * (seq + 2 * SUBLANES), LANES), F32),
            pltpu.VMEM((seq, cblk), F32),
            slab, slab, slab, slab,
        ],
        compiler_params=pltpu.CompilerParams(
            dimension_semantics=("arbitrary", "arbitrary"), vmem_limit_bytes=VMEM_LIMIT),
        name="rg_core",
    )(x, mod, gn, w_in, w_in, conv_w, conv_b, wg, bg, lam, h0)


def _rope_pairs(v, tab):
    p = v * tab
    return p + pltpu.roll(p, QK_ROPE, axis=1)


def _mlaproj_kernel(x_ref, mod_ref, gn_ref, wd_ref, gq_ref, gkv_ref, wuq_ref, wk_ref, wv_ref, tab_ref,
                    q_ref, k_ref, v_ref, *cache_refs, q_scale):
    mod = mod_ref[0, 0]
    h = _modnorm(x_ref[...], gn_ref[...], mod[0:1], mod[1:2]).astype(BF16)
    proj = _dot(h, wd_ref[...])
    cq = _rms(proj[:, :Q_LORA], gq_ref[...])
    ckv = _rms(proj[:, Q_LORA:Q_LORA + KV_LORA], gkv_ref[...])
    kp2 = proj[:, Q_LORA + KV_LORA:]
    if cache_refs:
        ckv_ref, kpe_ref = cache_refs
        ckv_ref[...] = ckv
        kpe_ref[...] = kp2[:, :QK_ROPE]
    tab = tab_ref[...]
    lane = lax.broadcasted_iota(jnp.int32, kp2.shape, 1)
    kr = jnp.where(lane < QK_ROPE, _rope_pairs(kp2, tab), 0.0).astype(BF16)
    q = _dot(cq.astype(BF16), wuq_ref[...])
    ckvb = ckv.astype(BF16)
    kn = _dot(ckvb, wk_ref[...])
    v_ref[...] = _dot(ckvb, wv_ref[...]).astype(BF16)
    for hd in range(N_HEADS):
        lo = hd * HEAD_SLOT
        q_ref[:, lo:lo + QK_NOPE] = (q[:, lo:lo + QK_NOPE] * q_scale).astype(BF16)
        qr = _rope_pairs(q[:, lo + QK_NOPE:lo + HEAD_SLOT], tab) * q_scale
        q_ref[:, lo + QK_NOPE:lo + HEAD_SLOT] = qr.astype(BF16)
        k_ref[:, lo:lo + QK_NOPE] = kn[:, hd * QK_NOPE:(hd + 1) * QK_NOPE].astype(BF16)
        k_ref[:, lo + QK_NOPE:lo + HEAD_SLOT] = kr


def _mlaproj_call(x, mod, layer, row_of_tile, gn, wd, gq, gkv, wuq, wk, wv, tab, tab_tile, *,
                  tm, emit_cache):
    ntok, d = x.shape
    const = dict(pipeline_mode=pl.Buffered(1))
    out_specs = [
        pl.BlockSpec((tm, N_HEADS * HEAD_SLOT), lambda i: (i, 0)),
        pl.BlockSpec((tm, N_HEADS * HEAD_SLOT), lambda i: (i, 0)),
        pl.BlockSpec((tm, N_HEADS * V_DIM), lambda i: (i, 0)),
    ]
    out_shape = [
        jax.ShapeDtypeStruct((ntok, N_HEADS * HEAD_SLOT), BF16),
        jax.ShapeDtypeStruct((ntok, N_HEADS * HEAD_SLOT), BF16),
        jax.ShapeDtypeStruct((ntok, N_HEADS * V_DIM), BF16),
    ]
    if emit_cache:
        out_specs += [pl.BlockSpec((tm, KV_LORA), lambda i: (i, 0)),
                      pl.BlockSpec((tm, QK_ROPE), lambda i: (i, 0))]
        out_shape += [jax.ShapeDtypeStruct((ntok, KV_LORA), F32),
                      jax.ShapeDtypeStruct((ntok, QK_ROPE), F32)]
    return pl.pallas_call(
        functools.partial(_mlaproj_kernel, q_scale=float((QK_NOPE + QK_ROPE) ** -0.5 * LOG2_E)),
        grid=(ntok // tm,),
        in_specs=[
            pl.BlockSpec((tm, d), lambda i: (i, 0)),
            pl.BlockSpec((1, 1, 6, d), lambda i: (layer, row_of_tile(i), 0, 0)),
            pl.BlockSpec((1, d), lambda i: (0, 0)),
            pl.BlockSpec(wd.shape, lambda i: (0, 0), **const),
            pl.BlockSpec((1, Q_LORA), lambda i: (0, 0)),
            pl.BlockSpec((1, KV_LORA), lambda i: (0, 0)),
            pl.BlockSpec(wuq.shape, lambda i: (0, 0), **const),
            pl.BlockSpec(wk.shape, lambda i: (0, 0), **const),
            pl.BlockSpec(wv.shape, lambda i: (0, 0), **const),
            pl.BlockSpec((tm, LANES), lambda i: (tab_tile(i), 0)),
        ],
        out_specs=out_specs,
        out_shape=out_shape,
        compiler_params=pltpu.CompilerParams(
            dimension_semantics=("arbitrary",), vmem_limit_bytes=VMEM_LIMIT),
        name="mla_proj",
    )(x, mod, gn, wd, gq, gkv, wuq, wk, wv, tab)


def _cachekv_kernel(ckv_ref, kpe_ref, wk_ref, wv_ref, k_ref, v_ref):
    ckvb = ckv_ref[...].astype(BF16)
    kn = _dot(ckvb, wk_ref[...])
    v_ref[...] = _dot(ckvb, wv_ref[...]).astype(BF16)
    kp = kpe_ref[...].astype(BF16)
    for hd in range(N_HEADS):
        lo = hd * HEAD_SLOT
        k_ref[:, lo:lo + QK_NOPE] = kn[:, hd * QK_NOPE:(hd + 1) * QK_NOPE].astype(BF16)
        k_ref[:, lo + QK_NOPE:lo + HEAD_SLOT] = kp


def _cachekv_call(ckv, kpe_padded, wk, wv, *, tm):
    ntok = ckv.shape[0]
    return pl.pallas_call(
        _cachekv_kernel,
        grid=(ntok // tm,),
        in_specs=[
            pl.BlockSpec((tm, KV_LORA), lambda i: (i, 0)),
            pl.BlockSpec((tm, LANES), lambda i: (i, 0)),
            pl.BlockSpec(wk.shape, lambda i: (0, 0)),
            pl.BlockSpec(wv.shape, lambda i: (0, 0)),
        ],
        out_specs=[
            pl.BlockSpec((tm, N_HEADS * HEAD_SLOT), lambda i: (i, 0)),
            pl.BlockSpec((tm, N_HEADS * V_DIM), lambda i: (i, 0)),
        ],
        out_shape=[
            jax.ShapeDtypeStruct((ntok, N_HEADS * HEAD_SLOT), BF16),
            jax.ShapeDtypeStruct((ntok, N_HEADS * V_DIM), BF16),
        ],
        compiler_params=pltpu.CompilerParams(
            dimension_semantics=("arbitrary",), vmem_limit_bytes=VMEM_LIMIT),
        name="mla_cache_kv",
    )(ckv, kpe_padded, wk, wv)


def _qk(q, k):
    return lax.dot_general(q, k, (((1,), (1,)), ((), ())), preferred_element_type=F32)


def _attn_kernel(q_ref, k_ref, v_ref, *rest):
    if len(rest) == 3:
        kc_ref, vc_ref, o_ref = rest
    else:
        kc_ref = vc_ref = None
        (o_ref,) = rest
    for hd in range(N_HEADS):
        slot = slice(hd * HEAD_SLOT, (hd + 1) * HEAD_SLOT)
        vcol = slice(hd * V_DIM, (hd + 1) * V_DIM)
        qh = q_ref[:, slot]
        s = _qk(qh, k_ref[0, :, slot])
        m = jnp.max(s, axis=-1, keepdims=True)
        if kc_ref is not None:
            sc = _qk(qh, kc_ref[0, :, slot])
            m = jnp.maximum(m, jnp.max(sc, axis=-1, keepdims=True))
        p = jnp.exp2(s - m)
        l = jnp.sum(p, axis=-1, keepdims=True)
        pv = _dot(p.astype(BF16), v_ref[0, :, vcol])
        if kc_ref is not None:
            pc = jnp.exp2(sc - m)
            l = l + jnp.sum(pc, axis=-1, keepdims=True)
            pv = pv + _dot(pc.astype(BF16), vc_ref[0, :, vcol])
        o_ref[:, vcol] = (pv * (1.0 / l)).astype(BF16)


def _attn_call(q, k, v, kc, vc, *, batch, tq):
    ntok = q.shape[0]
    seq = ntok // batch
    nq = seq // tq
    kw = N_HEADS * HEAD_SLOT
    vw = N_HEADS * V_DIM
    args = [q, k.reshape(batch, seq, kw), v.reshape(batch, seq, vw)]
    in_specs = [
        pl.BlockSpec((tq, kw), lambda b, i: (b * nq + i, 0)),
        pl.BlockSpec((1, seq, kw), lambda b, i: (b, 0, 0)),
        pl.BlockSpec((1, seq, vw), lambda b, i: (b, 0, 0)),
    ]
    if kc is not None:
        past = kc.shape[0] // batch
        args += [kc.reshape(batch, past, kw), vc.reshape(batch, past, vw)]
        in_specs += [pl.BlockSpec((1, past, kw), lambda b, i: (b, 0, 0)),
                     pl.BlockSpec((1, past, vw), lambda b, i: (b, 0, 0))]
    return pl.pallas_call(
        _attn_kernel,
        grid=(batch, nq),
        in_specs=in_specs,
        out_specs=pl.BlockSpec((tq, vw), lambda b, i: (b * nq + i, 0)),
        out_shape=jax.ShapeDtypeStruct((ntok, vw), BF16),
        compiler_params=pltpu.CompilerParams(
            dimension_semantics=("arbitrary", "arbitrary"), vmem_limit_bytes=VMEM_LIMIT),
        name="mla_attn",
    )(*args)


def _pair_partner(w):
    k, n = w.shape
    wp = w.reshape(k, n // 2, 2)
    return jnp.stack([-wp[..., 1], wp[..., 0]], axis=-1).reshape(k, n)


def _rope_table(n):
    rows = n // GRID_W
    row = jnp.repeat(jnp.arange(rows, dtype=F32), GRID_W)
    col = jnp.tile(jnp.arange(GRID_W, dtype=F32), rows)
    half = QK_ROPE // 2
    inv = ROPE_THETA ** (-jnp.arange(0, half, 2, dtype=F32) / half)
    ang = jnp.concatenate([row[:, None] * inv, col[:, None] * inv], axis=-1)
    ang = jnp.repeat(ang, 2, axis=-1)
    return jnp.concatenate([jnp.cos(ang), jnp.sin(ang)], axis=-1)


def _identity_rope_table(n):
    return jnp.concatenate([jnp.ones((n, QK_ROPE), F32), jnp.zeros((n, QK_ROPE), F32)], axis=-1)


def kernel(x_prompt, x_sample, state_rglru, cache_ckv, cache_kpe, c, c_ctx, ada_w, ada_b, norm_mix, norm_mlp, mlp_w1, mlp_w2, rg_w_in, rg_conv_w, rg_conv_b, rg_wa, rg_ba, rg_wx, rg_bx, rg_lambda, rg_w_out, mla_w_dqkv, mla_norm_q, mla_norm_kv, mla_w_uq, mla_w_ukv, mla_w_o, final_norm):
    batch, seq, d = x_prompt.shape
    dec_batch, dec_seq, _ = x_sample.shape
    depth = ada_w.shape[0]
    d_rnn = rg_lambda.shape[-1]
    past = cache_ckv.shape[2]
    assert dec_batch == COND_ROWS - SAMPLE_ROW0

    cond = jnp.concatenate(
        [jnp.broadcast_to(c_ctx[None, :], (SAMPLE_ROW0, d)), c], axis=0)
    mod = _ada_call(cond, ada_w, ada_b).reshape(depth, COND_ROWS, 6, d)

    w1 = mlp_w1.astype(BF16)
    w2 = mlp_w2.astype(BF16)
    w_in = rg_w_in.astype(BF16)
    w_out = rg_w_out.astype(BF16)
    w_o = mla_w_o.astype(BF16)
    n_rg, n_mla = rg_w_in.shape[0], mla_w_dqkv.shape[0]
    nblk = d_rnn // RG_BS
    wg = (0.5 * jnp.concatenate([rg_wa[:, 0], rg_wa[:, 1], rg_wx[:, 0], rg_wx[:, 1]], axis=-1)).astype(BF16)
    bg = 0.5 * jnp.concatenate(
        [rg_ba[:, 0].reshape(n_rg, nblk, 1, RG_BS), rg_ba[:, 1].reshape(n_rg, nblk, 1, RG_BS),
         rg_bx[:, 0].reshape(n_rg, nblk, 1, RG_BS), rg_bx[:, 1].reshape(n_rg, nblk, 1, RG_BS)], axis=-1)
    lo = Q_LORA + KV_LORA
    wd, wuq, wk, wv = [], [], [], []
    for j in range(n_mla):
        w = mla_w_dqkv[j]
        wd.append(jnp.concatenate([w, _pair_partner(w[:, lo:])], axis=1).astype(BF16))
        wq = mla_w_uq[j].reshape(Q_LORA, N_HEADS, QK_NOPE + QK_ROPE)
        wq_pe = wq[..., QK_NOPE:]
        wq_pp = jnp.stack([_pair_partner(wq_pe[:, hd]) for hd in range(N_HEADS)], axis=1)
        wuq.append(jnp.concatenate([wq, wq_pp], axis=-1).reshape(Q_LORA, N_HEADS * HEAD_SLOT).astype(BF16))
        wkv = mla_w_ukv[j].reshape(KV_LORA, N_HEADS, QK_NOPE + V_DIM)
        wk.append(wkv[..., :QK_NOPE].reshape(KV_LORA, N_HEADS * QK_NOPE).astype(BF16))
        wv.append(wkv[..., QK_NOPE:].reshape(KV_LORA, N_HEADS * V_DIM).astype(BF16))

    fn = final_norm.reshape(1, d)
    tab_sample = _rope_table(dec_seq)
    tm_ctx = 512
    tm_dec = 512
    tab_ctx = _identity_rope_table(tm_ctx)

    def run_stream(x, h0_all, cache, *, sample):
        b, t, _ = x.shape
        tm = tm_dec if sample else tm_ctx
        if sample:
            row_of_batch = lambda i: SAMPLE_ROW0 + i
            row_of_tile = lambda i: SAMPLE_ROW0 + i // (t // tm)
            tab, tab_tile = tab_sample, (lambda i: i % (t // tm))
        else:
            row_of_batch = lambda i: 0
            row_of_tile = lambda i: 0
            tab, tab_tile = tab_ctx, (lambda i: 0)
        xt = x.reshape(b * t, d)
        states, ckvs, kpes = [], [], []
        for l in range(depth):
            j = l // 2
            if l % 2 == 0:
                y, fin = _rg_call(xt.reshape(b, t, d), mod, l, row_of_batch, norm_mix[l].reshape(1, d),
                                  w_in[j], rg_conv_w[j], rg_conv_b[j].reshape(1, d_rnn), wg[j], bg[j],
                                  rg_lambda[j], h0_all[:, j], cblk=256)
                states.append(fin)
                y = y.reshape(b * t, d_rnn)
                wp = w_out[j]
            else:
                outs = _mlaproj_call(xt, mod, l, row_of_tile, norm_mix[l].reshape(1, d), wd[j],
                                     mla_norm_q[j].reshape(1, Q_LORA), mla_norm_kv[j].reshape(1, KV_LORA),
                                     wuq[j], wk[j], wv[j], tab, tab_tile, tm=tm, emit_cache=not sample)
                q, k, v = outs[:3]
                if sample:
                    ckv_c, kpe_c = cache
                    kpe_p = jnp.pad(kpe_c[:, j].reshape(b * past, QK_ROPE), ((0, 0), (0, LANES - QK_ROPE)))
                    kc, vc = _cachekv_call(ckv_c[:, j].reshape(b * past, KV_LORA), kpe_p, wk[j], wv[j], tm=512)
                else:
                    kc = vc = None
                    ckvs.append(outs[3].reshape(b, t, KV_LORA))
                    kpes.append(outs[4].reshape(b, t, QK_ROPE))
                y = _attn_call(q, k, v, kc, vc, batch=b, tq=256)
                wp = w_o[j]
            xt = _post_call(xt, y, mod, l, row_of_tile, norm_mlp[l].reshape(1, d), wp, w1[l], w2[l], fn,
                            tm=tm, final=(l == depth - 1))
        return xt.reshape(b, t, d), states, ckvs, kpes

    zeros_h0 = jnp.zeros((batch, n_rg, 2, d_rnn), F32)
    y_prompt, states, ckvs, kpes = run_stream(x_prompt, zeros_h0, None, sample=False)
    y_sample, _, _, _ = run_stream(x_sample, state_rglru, (cache_ckv, cache_kpe), sample=True)
    return (y_prompt, y_sample, jnp.stack(states, axis=1), jnp.stack(ckvs, axis=1), jnp.stack(kpes, axis=1))
```

```python
import functools

import jax
import jax.numpy as jnp
from jax import lax
from jax.experimental import pallas as pl
from jax.experimental.pallas import tpu as pltpu

F32 = jnp.float32
BF16 = jnp.bfloat16

EPS = 1e-6
LOG2_E = 1.4426950408889634
RG_C = 8.0
RG_BS = 128
N_HEADS = 8
QK_NOPE = 128
QK_ROPE = 64
V_DIM = 128
Q_LORA = 512
KV_LORA = 256
GRID_W = 64
ROPE_THETA = 10000.0
HEAD_SLOT = 256

LANES = 128
SUBLANES = 8
VMEM_LIMIT = 56 * 1024 * 1024

COND_ROWS = 16
SAMPLE_ROW0 = 8


def _dot(a, b):
    return jnp.dot(a, b, preferred_element_type=F32)


def _dot_nt(a, b):
    return lax.dot_general(a, b, (((1,), (1,)), ((), ())), preferred_element_type=F32)


def _rms(x, g):
    ms = jnp.mean(x * x, axis=-1, keepdims=True)
    return x * lax.rsqrt(ms + EPS) * g


def _modnorm(x, g, shift, scale):
    ms = jnp.mean(x * x, axis=-1, keepdims=True)
    return x * lax.rsqrt(ms + EPS) * (g * (1.0 + scale)) + shift


def _sigmoid(x):
    return 0.5 * jnp.tanh(0.5 * x) + 0.5


def _gelu_tanh(x):
    c = 0.7978845608028654
    xh = 0.5 * x
    return xh + xh * jnp.tanh(x * (c + (c * 0.044715) * (x * x)))


def _ada_kernel(cond_ref, w_ref, b_ref, o_ref):
    c = cond_ref[...]
    s = c * _sigmoid(c)
    o_ref[0] = jnp.dot(s, w_ref[0], preferred_element_type=F32,
                       precision=lax.Precision.HIGHEST) + b_ref[0]


def _ada_call(cond, ada_w, ada_b):
    depth, d, n6 = ada_w.shape
    tn = 1536
    return pl.pallas_call(
        _ada_kernel,
        grid=(depth, n6 // tn),
        in_specs=[
            pl.BlockSpec((COND_ROWS, d), lambda l, n: (0, 0)),
            pl.BlockSpec((1, d, tn), lambda l, n: (l, 0, n)),
            pl.BlockSpec((1, 1, tn), lambda l, n: (l, 0, n)),
        ],
        out_specs=pl.BlockSpec((1, COND_ROWS, tn), lambda l, n: (l, 0, n)),
        out_shape=jax.ShapeDtypeStruct((depth, COND_ROWS, n6), F32),
        compiler_params=pltpu.CompilerParams(
            dimension_semantics=("arbitrary", "arbitrary"), vmem_limit_bytes=VMEM_LIMIT),
        name="ada",
    )(cond, ada_w, ada_b.reshape(depth, 1, n6))


def _post_kernel(x_ref, y_ref, mod_ref, gn_ref, wp_ref, w1_ref, w2_ref, fn_ref, o_ref, *,
                 ff_chunk, final):
    mod = mod_ref[0, 0]
    g1, sh2, sc2, g2 = mod[2:3], mod[3:4], mod[4:5], mod[5:6]
    x1 = x_ref[...] + g1 * _dot(y_ref[...], wp_ref[...])
    h = _modnorm(x1, gn_ref[...], sh2, sc2).astype(BF16)
    ff = w1_ref.shape[1]
    acc = jnp.zeros_like(x1)
    for c in range(ff // ff_chunk):
        a = jnp.maximum(_dot(h, w1_ref[:, c * ff_chunk:(c + 1) * ff_chunk]), 0.0)
        acc = acc + _dot((a * a).astype(BF16), w2_ref[c * ff_chunk:(c + 1) * ff_chunk, :])
    out = x1 + g2 * acc
    if final:
        out = _rms(out, fn_ref[...])
    o_ref[...] = out


def _post_call(x, y, mod, layer, row_of_tile, gn, wp, w1, w2, fn, *, tm, final):
    ntok, d = x.shape
    ff = w1.shape[1]
    const = dict(pipeline_mode=pl.Buffered(1))
    return pl.pallas_call(
        functools.partial(_post_kernel, ff_chunk=1024, final=final),
        grid=(ntok // tm,),
        in_specs=[
            pl.BlockSpec((tm, d), lambda i: (i, 0)),
            pl.BlockSpec((tm, d), lambda i: (i, 0)),
            pl.BlockSpec((1, 1, 6, d), lambda i: (layer, row_of_tile(i), 0, 0)),
            pl.BlockSpec((1, d), lambda i: (0, 0)),
            pl.BlockSpec((d, d), lambda i: (0, 0), **const),
            pl.BlockSpec((d, ff), lambda i: (0, 0), **const),
            pl.BlockSpec((ff, d), lambda i: (0, 0), **const),
            pl.BlockSpec((1, d), lambda i: (0, 0)),
        ],
        out_specs=pl.BlockSpec((tm, d), lambda i: (i, 0)),
        out_shape=jax.ShapeDtypeStruct((ntok, d), F32),
        compiler_params=pltpu.CompilerParams(
            dimension_semantics=("arbitrary",), vmem_limit_bytes=VMEM_LIMIT),
        name="post",
    )(x, y, mod, gn, wp, w1, w2, fn)


def _rg_kernel(x_ref, mod_ref, gn_ref, wx_ref, wy_ref, cw_ref, cb_ref, wg_ref, bg_ref, lam_ref,
               h0_ref, y_ref, fin_ref,
               h_scr, xb_scr, yb_scr, af_scr, bf_scr, ab_scr, bb_scr, *,
               seq, seg, chunk):
    n_slab = wx_ref.shape[1] // LANES
    n_chunk = seq // chunk
    pad = SUBLANES * seg - seq
    halo = SUBLANES

    @pl.when(pl.program_id(1) == 0)
    def _():
        mod = mod_ref[0, 0]
        for c in range(n_chunk):
            rows = slice(c * chunk, (c + 1) * chunk)
            h_scr[rows, :] = _modnorm(x_ref[0, rows, :], gn_ref[...], mod[0:1], mod[1:2]).astype(BF16)

    def xb_rows(t0, n):
        return pl.ds(2 * (halo + t0), n, stride=2)

    for s in range(n_slab):
        xb_scr[s, 0:2 * halo, :] = jnp.zeros((2 * halo, LANES), F32)
        xb_scr[s, 2 * (halo + seq):2 * (2 * halo + seq), :] = jnp.zeros((2 * halo, LANES), F32)
    for c in range(n_chunk):
        rows = slice(c * chunk, (c + 1) * chunk)
        hc = h_scr[rows, :]
        ux = _dot(hc, wx_ref[...])
        for s in range(n_slab):
            xb_scr[s, xb_rows(c * chunk, chunk), :] = ux[:, s * LANES:(s + 1) * LANES]
        yb_scr[rows, :] = _gelu_tanh(_dot(hc, wy_ref[...]))

    for s in range(n_slab):
        tail = slice(seq, seq + pad)
        ones = jnp.ones((pad, LANES), F32)
        zeros = jnp.zeros((pad, LANES), F32)
        af_scr[s, tail, :] = ones
        ab_scr[s, tail, :] = ones
        bf_scr[s, tail, :] = zeros
        bb_scr[s, tail, :] = zeros

    cw = cw_ref[...]
    for s in range(n_slab):
        lanes = slice(s * LANES, (s + 1) * LANES)
        c1 = []
        for d in range(2):
            lam = lam_ref[d:d + 1, lanes]
            sp = jnp.maximum(-lam, 0.0) + jnp.log1p(jnp.exp(-jnp.abs(lam)))
            c1.append((-0.5 * RG_C * LOG2_E) * sp)
        for c in range(n_chunk):
            rows = slice(c * chunk, (c + 1) * chunk)
            xs = cb_ref[:, lanes] + cw[0:1, lanes] * xb_scr[s, xb_rows(c * chunk - 1, chunk), :]
            for k in range(1, 4):
                xs = xs + cw[k:k + 1, lanes] * xb_scr[s, xb_rows(c * chunk - 1 + k, chunk), :]
            g = _dot(xs.astype(BF16), wg_ref[s]) + bg_ref[s]
            xh = 0.5 * xs
            for d, (a_scr, b_scr) in enumerate(((af_scr, bf_scr), (ab_scr, bb_scr))):
                tr = jnp.tanh(g[:, d * LANES:(d + 1) * LANES])
                ti = jnp.tanh(g[:, (2 + d) * LANES:(3 + d) * LANES])
                a = jnp.exp2(c1[d] * tr + c1[d])
                a_scr[s, rows, :] = a
                v = 1.0 - a * a
                root = jnp.where(v > 0.0, v * lax.rsqrt(v), 0.0)
                b_scr[s, rows, :] = root * (ti * xh + xh)

    def seg_rows(j):
        return pl.ds(j, SUBLANES, stride=seg)

    def run_steps(step, init):
        unroll = 4
        main = seg // unroll

        def body(i, carry):
            for u in range(unroll):
                carry = step(i * unroll + u, carry)
            return carry

        carry = lax.fori_loop(0, main, body, init)
        for j in range(main * unroll, seg):
            carry = step(j, carry)
        return carry

    def pass1(j, carry):
        jb = seg - 1 - j
        out = []
        for s in range(n_slab):
            pf, hf, pb, hb = carry[s]
            a = af_scr[s, seg_rows(j), :]
            b = bf_scr[s, seg_rows(j), :]
            a2 = ab_scr[s, seg_rows(jb), :]
            b2 = bb_scr[s, seg_rows(jb), :]
            out.append((pf * a, a * hf + b, pb * a2, a2 * hb + b2))
        return tuple(out)

    one = jnp.ones((SUBLANES, LANES), F32)
    zero = jnp.zeros((SUBLANES, LANES), F32)
    totals = run_steps(pass1, tuple((one, zero, one, zero) for _ in range(n_slab)))

    rid = lax.broadcasted_iota(jnp.int32, (SUBLANES, LANES), 0)
    init_f, init_b = [], []
    for s in range(n_slab):
        lanes = slice(s * LANES, (s + 1) * LANES)
        pf, hf, pb, hb = totals[s]
        st = h0_ref[0, 0:1, lanes]
        ini = zero
        for k in range(SUBLANES):
            ini = jnp.where(rid == k, st, ini)
            st = hf[k:k + 1] + pf[k:k + 1] * st
        init_f.append(ini)
        fin_ref[0, 0:1, lanes] = st
        st = h0_ref[0, 1:2, lanes]
        ini = zero
        for k in range(SUBLANES - 1, -1, -1):
            ini = jnp.where(rid == k, st, ini)
            st = hb[k:k + 1] + pb[k:k + 1] * st
        init_b.append(ini)
        fin_ref[0, 1:2, lanes] = st

    def pass2(j, carry):
        jb = seg - 1 - j
        out = []
        for s in range(n_slab):
            hf, hb = carry[s]
            hf = af_scr[s, seg_rows(j), :] * hf + bf_scr[s, seg_rows(j), :]
            bf_scr[s, seg_rows(j), :] = hf
            hb = ab_scr[s, seg_rows(jb), :] * hb + bb_scr[s, seg_rows(jb), :]
            bb_scr[s, seg_rows(jb), :] = hb
            out.append((hf, hb))
        return tuple(out)

    run_steps(pass2, tuple(zip(init_f, init_b)))

    for c in range(n_chunk):
        rows = slice(c * chunk, (c + 1) * chunk)
        for s in range(n_slab):
            lanes = slice(s * LANES, (s + 1) * LANES)
            y_ref[0, rows, lanes] = ((bf_scr[s, rows, :] + bb_scr[s, rows, :]) * yb_scr[rows, lanes]).astype(BF16)


def _odd_segment_length(seq):
    seg = -(-seq // SUBLANES)
    return seg if seg % 2 == 1 else seg + 1


def _rg_call(x, mod, layer, row_of_batch, gn, w_in, conv_w, conv_b, wg, bg, lam, h0, *, cblk):
    b, seq, d = x.shape
    d_rnn = lam.shape[1]
    ncb = d_rnn // cblk
    n_slab = cblk // LANES
    seg = _odd_segment_length(seq)
    seq_pad = SUBLANES * seg
    chunk = min(seq, 512)
    slab = pltpu.VMEM((n_slab, seq_pad, LANES), F32)
    return pl.pallas_call(
        functools.partial(_rg_kernel, seq=seq, seg=seg, chunk=chunk),
        grid=(b, ncb),
        in_specs=[
            pl.BlockSpec((1, seq, d), lambda i, c: (i, 0, 0)),
            pl.BlockSpec((1, 1, 6, d), lambda i, c: (layer, row_of_batch(i), 0, 0)),
            pl.BlockSpec((1, d), lambda i, c: (0, 0)),
            pl.BlockSpec((d, cblk), lambda i, c: (0, c)),
            pl.BlockSpec((d, cblk), lambda i, c: (0, ncb + c)),
            pl.BlockSpec((4, cblk), lambda i, c: (0, c)),
            pl.BlockSpec((1, cblk), lambda i, c: (0, c)),
            pl.BlockSpec((n_slab, RG_BS, 4 * RG_BS), lambda i, c: (c, 0, 0)),
            pl.BlockSpec((n_slab, 1, 4 * RG_BS), lambda i, c: (c, 0, 0)),
            pl.BlockSpec((2, cblk), lambda i, c: (0, c)),
            pl.BlockSpec((1, 2, cblk), lambda i, c: (i, 0, c)),
        ],
        out_specs=[
            pl.BlockSpec((1, seq, cblk), lambda i, c: (i, 0, c)),
            pl.BlockSpec((1, 2, cblk), lambda i, c: (i, 0, c)),
        ],
        out_shape=[
            jax.ShapeDtypeStruct((b, seq, d_rnn), BF16),
            jax.ShapeDtypeStruct((b, 2, d_rnn), F32),
        ],
        scratch_shapes=[
            pltpu.VMEM((seq, d), BF16),
            pltpu.VMEM((n_slab, 2 * (seq + 2 * SUBLANES), LANES), F32),
            pltpu.VMEM((seq, cblk), F32),
            slab, slab, slab, slab,
        ],
        compiler_params=pltpu.CompilerParams(
            dimension_semantics=("arbitrary", "arbitrary"), vmem_limit_bytes=VMEM_LIMIT),
        name="rg_core",
    )(x, mod, gn, w_in, w_in, conv_w, conv_b, wg, bg, lam, h0)


def _rope_pairs(v, tab):
    p = v * tab
    return p + pltpu.roll(p, QK_ROPE, axis=1)


def _mlaproj_kernel(x_ref, mod_ref, gn_ref, wd_ref, gq_ref, gkv_ref, wuq_ref, wk_ref, wv_ref, tab_ref,
                    tabt_ref, q_ref, k_ref, v_ref, *cache_refs, q_scale):
    mod = mod_ref[0, 0]
    h = _modnorm(x_ref[...], gn_ref[...], mod[0:1], mod[1:2]).astype(BF16)
    proj = _dot(h, wd_ref[...])
    cq = _rms(proj[:, :Q_LORA], gq_ref[...])
    ckv = _rms(proj[:, Q_LORA:Q_LORA + KV_LORA], gkv_ref[...])
    kp2 = proj[:, Q_LORA + KV_LORA:]
    if cache_refs:
        ckv_ref, kpe_ref = cache_refs
        ckv_ref[...] = ckv
        kpe_ref[...] = kp2[:, :QK_ROPE]
    tab = tab_ref[...]
    lane = lax.broadcasted_iota(jnp.int32, kp2.shape, 1)
    kr = jnp.where(lane < QK_ROPE, _rope_pairs(kp2, tab), 0.0).astype(BF16)
    qt = _dot_nt(wuq_ref[...], cq.astype(BF16))
    ckvb = ckv.astype(BF16)
    kn = _dot(ckvb, wk_ref[...])
    v_ref[0] = _dot_nt(wv_ref[...], ckvb).astype(BF16)
    tabt = tabt_ref[...] * q_scale
    for hd in range(N_HEADS):
        lo = hd * HEAD_SLOT
        q_ref[0, lo:lo + QK_NOPE, :] = (qt[lo:lo + QK_NOPE] * q_scale).astype(BF16)
        pr = qt[lo + QK_NOPE:lo + HEAD_SLOT] * tabt
        qr = (pr[:QK_ROPE] + pr[QK_ROPE:]).astype(BF16)
        q_ref[0, lo + QK_NOPE:lo + QK_NOPE + QK_ROPE, :] = qr
        q_ref[0, lo + QK_NOPE + QK_ROPE:lo + HEAD_SLOT, :] = qr
        k_ref[:, lo:lo + QK_NOPE] = kn[:, hd * QK_NOPE:(hd + 1) * QK_NOPE].astype(BF16)
        k_ref[:, lo + QK_NOPE:lo + HEAD_SLOT] = kr


def _mlaproj_call(x, mod, layer, row_of_tile, gn, wd, gq, gkv, wuq_t, wk, wv_t, tab, tab_t, tab_tile, *,
                  seq, tm, emit_cache):
    ntok, d = x.shape
    nt = seq // tm
    const = dict(pipeline_mode=pl.Buffered(1))
    out_specs = [
        pl.BlockSpec((1, N_HEADS * HEAD_SLOT, tm), lambda i: (i // nt, 0, i % nt)),
        pl.BlockSpec((tm, N_HEADS * HEAD_SLOT), lambda i: (i, 0)),
        pl.BlockSpec((1, N_HEADS * V_DIM, tm), lambda i: (i // nt, 0, i % nt)),
    ]
    out_shape = [
        jax.ShapeDtypeStruct((ntok // seq, N_HEADS * HEAD_SLOT, seq), BF16),
        jax.ShapeDtypeStruct((ntok, N_HEADS * HEAD_SLOT), BF16),
        jax.ShapeDtypeStruct((ntok // seq, N_HEADS * V_DIM, seq), BF16),
    ]
    if emit_cache:
        out_specs += [pl.BlockSpec((tm, KV_LORA), lambda i: (i, 0)),
                      pl.BlockSpec((tm, QK_ROPE), lambda i: (i, 0))]
        out_shape += [jax.ShapeDtypeStruct((ntok, KV_LORA), F32),
                      jax.ShapeDtypeStruct((ntok, QK_ROPE), F32)]
    return pl.pallas_call(
        functools.partial(_mlaproj_kernel, q_scale=float((QK_NOPE + QK_ROPE) ** -0.5 * LOG2_E)),
        grid=(ntok // tm,),
        in_specs=[
            pl.BlockSpec((tm, d), lambda i: (i, 0)),
            pl.BlockSpec((1, 1, 6, d), lambda i: (layer, row_of_tile(i), 0, 0)),
            pl.BlockSpec((1, d), lambda i: (0, 0)),
            pl.BlockSpec(wd.shape, lambda i: (0, 0), **const),
            pl.BlockSpec((1, Q_LORA), lambda i: (0, 0)),
            pl.BlockSpec((1, KV_LORA), lambda i: (0, 0)),
            pl.BlockSpec(wuq_t.shape, lambda i: (0, 0), **const),
            pl.BlockSpec(wk.shape, lambda i: (0, 0), **const),
            pl.BlockSpec(wv_t.shape, lambda i: (0, 0), **const),
            pl.BlockSpec((tm, LANES), lambda i: (tab_tile(i), 0)),
            pl.BlockSpec((LANES, tm), lambda i: (0, tab_tile(i))),
        ],
        out_specs=out_specs,
        out_shape=out_shape,
        compiler_params=pltpu.CompilerParams(
            dimension_semantics=("arbitrary",), vmem_limit_bytes=VMEM_LIMIT),
        name="mla_proj",
    )(x, mod, gn, wd, gq, gkv, wuq_t, wk, wv_t, tab, tab_t)


def _cachekv_kernel(ckv_ref, kpe_ref, wk_ref, wv_ref, k_ref, v_ref):
    ckvb = ckv_ref[...].astype(BF16)
    kn = _dot(ckvb, wk_ref[...])
    v_ref[0] = _dot_nt(wv_ref[...], ckvb).astype(BF16)
    kp = kpe_ref[...].astype(BF16)
    for hd in range(N_HEADS):
        lo = hd * HEAD_SLOT
        k_ref[:, lo:lo + QK_NOPE] = kn[:, hd * QK_NOPE:(hd + 1) * QK_NOPE].astype(BF16)
        k_ref[:, lo + QK_NOPE:lo + HEAD_SLOT] = kp


def _cachekv_call(ckv, kpe_padded, wk, wv_t, *, past):
    ntok = ckv.shape[0]
    return pl.pallas_call(
        _cachekv_kernel,
        grid=(ntok // past,),
        in_specs=[
            pl.BlockSpec((past, KV_LORA), lambda i: (i, 0)),
            pl.BlockSpec((past, LANES), lambda i: (i, 0)),
            pl.BlockSpec(wk.shape, lambda i: (0, 0)),
            pl.BlockSpec(wv_t.shape, lambda i: (0, 0)),
        ],
        out_specs=[
            pl.BlockSpec((past, N_HEADS * HEAD_SLOT), lambda i: (i, 0)),
            pl.BlockSpec((1, N_HEADS * V_DIM, past), lambda i: (i, 0, 0)),
        ],
        out_shape=[
            jax.ShapeDtypeStruct((ntok, N_HEADS * HEAD_SLOT), BF16),
            jax.ShapeDtypeStruct((ntok // past, N_HEADS * V_DIM, past), BF16),
        ],
        compiler_params=pltpu.CompilerParams(
            dimension_semantics=("arbitrary",), vmem_limit_bytes=VMEM_LIMIT),
        name="mla_cache_kv",
    )(ckv, kpe_padded, wk, wv_t)


def _attn_kernel(q_ref, k_ref, v_ref, *rest):
    if len(rest) == 4:
        kc_ref, vc_ref, o_ref, s_scr = rest
    else:
        kc_ref = vc_ref = None
        o_ref, s_scr = rest
    kchunk = 256
    chunks = [(k_ref, v_ref, r, r) for r in range(0, k_ref.shape[1], kchunk)]
    if kc_ref is not None:
        chunks += [(kc_ref, vc_ref, r, k_ref.shape[1] + r) for r in range(0, kc_ref.shape[1], kchunk)]

    def score_chunk(hd, chunk):
        kref, _, r, srow = chunk
        slot = slice(hd * HEAD_SLOT, (hd + 1) * HEAD_SLOT)
        s = _dot(kref[0, r:r + kchunk, slot], q_ref[0, slot, :])
        s_scr[hd % n_buf, srow:srow + kchunk, :] = s
        return jnp.max(s, axis=0, keepdims=True)

    def value_chunk(hd, chunk, m):
        _, vref, r, srow = chunk
        p = jnp.exp2(s_scr[hd % n_buf, srow:srow + kchunk, :] - m)
        ov = _dot(vref[0, hd * V_DIM:(hd + 1) * V_DIM, r:r + kchunk], p.astype(BF16))
        return jnp.sum(p, axis=0, keepdims=True), ov

    def running_max(acc, new):
        return new if acc is None else jnp.maximum(acc, new)

    n_buf = s_scr.shape[0]
    nc = len(chunks)
    ahead = nc + min(nc, 3)
    assert n_buf == 3 and nc < ahead <= 2 * nc
    total = N_HEADS * nc
    maxes = {}

    def issue_scores(i):
        hd, c = divmod(i, nc)
        maxes[hd] = running_max(maxes.get(hd), score_chunk(hd, chunks[c]))

    for i in range(ahead):
        issue_scores(i)
    l, ov = None, None
    for j in range(total):
        if j + ahead < total:
            issue_scores(j + ahead)
        hd, c = divmod(j, nc)
        l_c, ov_c = value_chunk(hd, chunks[c], maxes[hd])
        l = l_c if c == 0 else l + l_c
        ov = ov_c if c == 0 else ov + ov_c
        if c == nc - 1:
            o_ref[:, hd * V_DIM:(hd + 1) * V_DIM] = (ov * (1.0 / l)).T.astype(BF16)


def _attn_call(q_t, k, v_t, kc, vc_t, *, tq):
    batch, kw, seq = q_t.shape
    ntok = batch * seq
    nq = seq // tq
    vw = N_HEADS * V_DIM
    args = [q_t, k.reshape(batch, seq, kw), v_t]
    in_specs = [
        pl.BlockSpec((1, kw, tq), lambda b, i: (b, 0, i)),
        pl.BlockSpec((1, seq, kw), lambda b, i: (b, 0, 0)),
        pl.BlockSpec((1, vw, seq), lambda b, i: (b, 0, 0)),
    ]
    past = 0
    if kc is not None:
        past = kc.shape[0] // batch
        args += [kc.reshape(batch, past, kw), vc_t]
        in_specs += [pl.BlockSpec((1, past, kw), lambda b, i: (b, 0, 0)),
                     pl.BlockSpec((1, vw, past), lambda b, i: (b, 0, 0))]
    return pl.pallas_call(
        _attn_kernel,
        grid=(batch, nq),
        in_specs=in_specs,
        out_specs=pl.BlockSpec((tq, vw), lambda b, i: (b * nq + i, 0)),
        out_shape=jax.ShapeDtypeStruct((ntok, vw), BF16),
        scratch_shapes=[pltpu.VMEM((3, seq + past, tq), F32)],
        compiler_params=pltpu.CompilerParams(
            dimension_semantics=("arbitrary", "arbitrary"), vmem_limit_bytes=VMEM_LIMIT),
        name="mla_attn",
    )(*args)


def _pair_partner(w):
    k, n = w.shape
    wp = w.reshape(k, n // 2, 2)
    return jnp.stack([-wp[..., 1], wp[..., 0]], axis=-1).reshape(k, n)


def _rope_table(n):
    rows = n // GRID_W
    row = jnp.repeat(jnp.arange(rows, dtype=F32), GRID_W)
    col = jnp.tile(jnp.arange(GRID_W, dtype=F32), rows)
    half = QK_ROPE // 2
    inv = ROPE_THETA ** (-jnp.arange(0, half, 2, dtype=F32) / half)
    ang = jnp.concatenate([row[:, None] * inv, col[:, None] * inv], axis=-1)
    ang = jnp.repeat(ang, 2, axis=-1)
    return jnp.concatenate([jnp.cos(ang), jnp.sin(ang)], axis=-1)


def _identity_rope_table(n):
    return jnp.concatenate([jnp.ones((n, QK_ROPE), F32), jnp.zeros((n, QK_ROPE), F32)], axis=-1)


def kernel(x_prompt, x_sample, state_rglru, cache_ckv, cache_kpe, c, c_ctx, ada_w, ada_b, norm_mix, norm_mlp, mlp_w1, mlp_w2, rg_w_in, rg_conv_w, rg_conv_b, rg_wa, rg_ba, rg_wx, rg_bx, rg_lambda, rg_w_out, mla_w_dqkv, mla_norm_q, mla_norm_kv, mla_w_uq, mla_w_ukv, mla_w_o, final_norm):
    batch, seq, d = x_prompt.shape
    dec_batch, dec_seq, _ = x_sample.shape
    depth = ada_w.shape[0]
    d_rnn = rg_lambda.shape[-1]
    past = cache_ckv.shape[2]
    assert dec_batch == COND_ROWS - SAMPLE_ROW0

    cond = jnp.concatenate(
        [jnp.broadcast_to(c_ctx[None, :], (SAMPLE_ROW0, d)), c], axis=0)
    mod = _ada_call(cond, ada_w, ada_b).reshape(depth, COND_ROWS, 6, d)

    w1 = mlp_w1.astype(BF16)
    w2 = mlp_w2.astype(BF16)
    w_in = rg_w_in.astype(BF16)
    w_out = rg_w_out.astype(BF16)
    w_o = mla_w_o.astype(BF16)
    n_rg, n_mla = rg_w_in.shape[0], mla_w_dqkv.shape[0]
    nblk = d_rnn // RG_BS
    wg = (0.5 * jnp.concatenate([rg_wa[:, 0], rg_wa[:, 1], rg_wx[:, 0], rg_wx[:, 1]], axis=-1)).astype(BF16)
    bg = 0.5 * jnp.concatenate(
        [rg_ba[:, 0].reshape(n_rg, nblk, 1, RG_BS), rg_ba[:, 1].reshape(n_rg, nblk, 1, RG_BS),
         rg_bx[:, 0].reshape(n_rg, nblk, 1, RG_BS), rg_bx[:, 1].reshape(n_rg, nblk, 1, RG_BS)], axis=-1)
    lo = Q_LORA + KV_LORA
    wd, wuq, wk, wv = [], [], [], []
    for j in range(n_mla):
        w = mla_w_dqkv[j]
        wd.append(jnp.concatenate([w, _pair_partner(w[:, lo:])], axis=1).astype(BF16))
        wq = mla_w_uq[j].reshape(Q_LORA, N_HEADS, QK_NOPE + QK_ROPE)
        wq_pe = wq[..., QK_NOPE:]
        wq_pp = jnp.stack([_pair_partner(wq_pe[:, hd]) for hd in range(N_HEADS)], axis=1)
        wuq.append(jnp.concatenate([wq, wq_pp], axis=-1).reshape(Q_LORA, N_HEADS * HEAD_SLOT).T.astype(BF16))
        wkv = mla_w_ukv[j].reshape(KV_LORA, N_HEADS, QK_NOPE + V_DIM)
        wk.append(wkv[..., :QK_NOPE].reshape(KV_LORA, N_HEADS * QK_NOPE).astype(BF16))
        wv.append(wkv[..., QK_NOPE:].reshape(KV_LORA, N_HEADS * V_DIM).T.astype(BF16))

    fn = final_norm.reshape(1, d)
    tab_sample = _rope_table(dec_seq)
    tm = 512

    def run_stream(x, h0_all, cache, *, sample):
        b, t, _ = x.shape
        tmp = min(tm, t)
        if sample:
            row_of_batch = lambda i: SAMPLE_ROW0 + i
            row_of_tile = lambda i: SAMPLE_ROW0 + i // (t // tm)
            row_of_ptile = lambda i: SAMPLE_ROW0 + i // (t // tmp)
            tab, tab_tile = tab_sample, (lambda i: i % (t // tmp))
        else:
            row_of_batch = lambda i: 0
            row_of_tile = lambda i: 0
            row_of_ptile = lambda i: 0
            tab, tab_tile = _identity_rope_table(tmp), (lambda i: 0)
        tab_t = tab.T
        xt = x.reshape(b * t, d)
        states, ckvs, kpes = [], [], []
        for l in range(depth):
            j = l // 2
            if l % 2 == 0:
                y, fin = _rg_call(xt.reshape(b, t, d), mod, l, row_of_batch, norm_mix[l].reshape(1, d),
                                  w_in[j], rg_conv_w[j], rg_conv_b[j].reshape(1, d_rnn), wg[j], bg[j],
                                  rg_lambda[j], h0_all[:, j], cblk=256)
                states.append(fin)
                y = y.reshape(b * t, d_rnn)
                wp = w_out[j]
            else:
                outs = _mlaproj_call(xt, mod, l, row_of_ptile, norm_mix[l].reshape(1, d), wd[j],
                                     mla_norm_q[j].reshape(1, Q_LORA), mla_norm_kv[j].reshape(1, KV_LORA),
                                     wuq[j], wk[j], wv[j], tab, tab_t, tab_tile, seq=t, tm=tmp,
                                     emit_cache=not sample)
                q, k, v = outs[:3]
                if sample:
                    ckv_c, kpe_c = cache
                    kpe_p = jnp.pad(kpe_c[:, j].reshape(b * past, QK_ROPE), ((0, 0), (0, LANES - QK_ROPE)))
                    kc, vc = _cachekv_call(ckv_c[:, j].reshape(b * past, KV_LORA), kpe_p, wk[j], wv[j], past=past)
                else:
                    kc = vc = None
                    ckvs.append(outs[3].reshape(b, t, KV_LORA))
                    kpes.append(outs[4].reshape(b, t, QK_ROPE))
                y = _attn_call(q, k, v, kc, vc, tq=min(512, t))
                wp = w_o[j]
            xt = _post_call(xt, y, mod, l, row_of_tile, norm_mlp[l].reshape(1, d), wp, w1[l], w2[l], fn,
                            tm=tm, final=(l == depth - 1))
        return xt.reshape(b, t, d), states, ckvs, kpes

    zeros_h0 = jnp.zeros((batch, n_rg, 2, d_rnn), F32)
    y_prompt, states, ckvs, kpes = run_stream(x_prompt, zeros_h0, None, sample=False)
    y_sample, _, _, _ = run_stream(x_sample, state_rglru, (cache_ckv, cache_kpe), sample=True)
    return (y_prompt, y_sample, jnp.stack(states, axis=1), jnp.stack(ckvs, axis=1), jnp.stack(kpes, axis=1))
```

```python
import functools

import jax
import jax.numpy as jnp
from jax import lax
from jax.experimental import pallas as pl
from jax.experimental.pallas import tpu as pltpu

F32 = jnp.float32
BF16 = jnp.bfloat16

EPS = 1e-6
LOG2_E = 1.4426950408889634
RG_C = 8.0
RG_BS = 128
N_HEADS = 8
QK_NOPE = 128
QK_ROPE = 64
V_DIM = 128
Q_LORA = 512
KV_LORA = 256
GRID_W = 64
ROPE_THETA = 10000.0
HEAD_SLOT = 256

LANES = 128
SUBLANES = 8
VMEM_LIMIT = 56 * 1024 * 1024

COND_ROWS = 16
SAMPLE_ROW0 = 8


def _dot(a, b):
    return jnp.dot(a, b, preferred_element_type=F32)


def _dot_nt(a, b):
    return lax.dot_general(a, b, (((1,), (1,)), ((), ())), preferred_element_type=F32)


def _rms(x, g):
    ms = jnp.mean(x * x, axis=-1, keepdims=True)
    return x * lax.rsqrt(ms + EPS) * g


def _modnorm(x, g, shift, scale):
    ms = jnp.mean(x * x, axis=-1, keepdims=True)
    return x * lax.rsqrt(ms + EPS) * (g * (1.0 + scale)) + shift


def _sigmoid(x):
    return 0.5 * jnp.tanh(0.5 * x) + 0.5


def _gelu_tanh(x):
    c = 0.7978845608028654
    xh = 0.5 * x
    return xh + xh * jnp.tanh(x * (c + (c * 0.044715) * (x * x)))


def _ada_kernel(cond_ref, w_ref, b_ref, o_ref):
    c = cond_ref[...]
    s = c * _sigmoid(c)
    rows = s.shape[0]
    s_hi = s.astype(BF16)
    s_lo = (s - s_hi.astype(F32)).astype(BF16)
    w = w_ref[0]
    w_hi = w.astype(BF16)
    w_lo = (w - w_hi.astype(F32)).astype(BF16)
    main = _dot(jnp.concatenate([s_hi, s_lo], axis=0), w_hi)
    o_ref[0] = main[:rows] + main[rows:] + _dot(s_hi, w_lo) + b_ref[0]


def _ada_call(cond, ada_w, ada_b):
    depth, d, n6 = ada_w.shape
    tn = 1536
    return pl.pallas_call(
        _ada_kernel,
        grid=(depth, n6 // tn),
        in_specs=[
            pl.BlockSpec((COND_ROWS, d), lambda l, n: (0, 0)),
            pl.BlockSpec((1, d, tn), lambda l, n: (l, 0, n)),
            pl.BlockSpec((1, 1, tn), lambda l, n: (l, 0, n)),
        ],
        out_specs=pl.BlockSpec((1, COND_ROWS, tn), lambda l, n: (l, 0, n)),
        out_shape=jax.ShapeDtypeStruct((depth, COND_ROWS, n6), F32),
        compiler_params=pltpu.CompilerParams(
            dimension_semantics=("arbitrary", "arbitrary"), vmem_limit_bytes=VMEM_LIMIT),
        name="ada",
    )(cond, ada_w, ada_b.reshape(depth, 1, n6))


def _post_kernel(x_ref, y_ref, mod_ref, gn_ref, wp_ref, w1_ref, w2_ref, fn_ref, o_ref, *,
                 ff_chunk, final):
    mod = mod_ref[0, 0]
    g1, sh2, sc2, g2 = mod[2:3], mod[3:4], mod[4:5], mod[5:6]
    x1 = x_ref[...] + g1 * _dot(y_ref[...], wp_ref[...])
    h = _modnorm(x1, gn_ref[...], sh2, sc2).astype(BF16)
    ff = w1_ref.shape[1]
    acc = jnp.zeros_like(x1)
    for c in range(ff // ff_chunk):
        a = jnp.maximum(_dot(h, w1_ref[:, c * ff_chunk:(c + 1) * ff_chunk]), 0.0)
        acc = acc + _dot((a * a).astype(BF16), w2_ref[c * ff_chunk:(c + 1) * ff_chunk, :])
    out = x1 + g2 * acc
    if final:
        out = _rms(out, fn_ref[...])
    o_ref[...] = out


def _post_call(x, y, mod, layer, row_of_tile, gn, wp, wp_index, w1, w2, fn, *, tm, final):
    ntok, d = x.shape
    ff = w1.shape[2]
    const = dict(pipeline_mode=pl.Buffered(1))
    return pl.pallas_call(
        functools.partial(_post_kernel, ff_chunk=1024, final=final),
        grid=(ntok // tm,),
        in_specs=[
            pl.BlockSpec((tm, d), lambda i: (i, 0)),
            pl.BlockSpec((tm, d), lambda i: (i, 0)),
            pl.BlockSpec((1, 1, 6, d), lambda i: (layer, row_of_tile(i), 0, 0)),
            pl.BlockSpec((1, d), lambda i: (0, 0)),
            pl.BlockSpec((None, d, d), lambda i: (wp_index, 0, 0), **const),
            pl.BlockSpec((None, d, ff), lambda i: (layer, 0, 0), **const),
            pl.BlockSpec((None, ff, d), lambda i: (layer, 0, 0), **const),
            pl.BlockSpec((1, d), lambda i: (0, 0)),
        ],
        out_specs=pl.BlockSpec((tm, d), lambda i: (i, 0)),
        out_shape=jax.ShapeDtypeStruct((ntok, d), F32),
        compiler_params=pltpu.CompilerParams(
            dimension_semantics=("arbitrary",), vmem_limit_bytes=VMEM_LIMIT),
        name="post",
    )(x, y, mod, gn, wp, w1, w2, fn)


def _rg_kernel(x_ref, mod_ref, gn_ref, wx_ref, wy_ref, cw_ref, cb_ref, wg_ref, bg_ref, lam_ref,
               h0_ref, y_ref, fin_ref,
               h_scr, xb_scr, yb_scr, af_scr, bf_scr, ab_scr, bb_scr, hf_scr, hb_scr, *,
               seq, seg, chunk):
    n_slab = wx_ref.shape[1] // LANES
    n_chunk = seq // chunk
    pad = SUBLANES * seg - seq
    halo = SUBLANES

    @pl.when(pl.program_id(1) == 0)
    def _():
        mod = mod_ref[0, 0]
        for c in range(n_chunk):
            rows = slice(c * chunk, (c + 1) * chunk)
            h_scr[rows, :] = _modnorm(x_ref[0, rows, :], gn_ref[...], mod[0:1], mod[1:2]).astype(BF16)

    def xb_rows(t0, n):
        return pl.ds(2 * (halo + t0), n, stride=2)

    for s in range(n_slab):
        xb_scr[s, 0:2 * halo, :] = jnp.zeros((2 * halo, LANES), F32)
        xb_scr[s, 2 * (halo + seq):2 * (2 * halo + seq), :] = jnp.zeros((2 * halo, LANES), F32)
    for c in range(n_chunk):
        rows = slice(c * chunk, (c + 1) * chunk)
        hc = h_scr[rows, :]
        ux = _dot(hc, wx_ref[...])
        for s in range(n_slab):
            xb_scr[s, xb_rows(c * chunk, chunk), :] = ux[:, s * LANES:(s + 1) * LANES]
        yb_scr[rows, :] = _gelu_tanh(_dot(hc, wy_ref[...]))

    for s in range(n_slab):
        tail = slice(seq, seq + pad)
        ones = jnp.ones((pad, LANES), F32)
        zeros = jnp.zeros((pad, LANES), F32)
        af_scr[s, tail, :] = ones
        ab_scr[s, tail, :] = ones
        bf_scr[s, tail, :] = zeros
        bb_scr[s, tail, :] = zeros

    cw = cw_ref[...]
    for s in range(n_slab):
        lanes = slice(s * LANES, (s + 1) * LANES)
        c1 = []
        for d in range(2):
            lam = lam_ref[d:d + 1, lanes]
            sp = jnp.maximum(-lam, 0.0) + jnp.log1p(jnp.exp(-jnp.abs(lam)))
            c1.append((-0.5 * RG_C * LOG2_E) * sp)
        for c in range(n_chunk):
            rows = slice(c * chunk, (c + 1) * chunk)
            xs = cb_ref[:, lanes] + cw[0:1, lanes] * xb_scr[s, xb_rows(c * chunk - 1, chunk), :]
            for k in range(1, 4):
                xs = xs + cw[k:k + 1, lanes] * xb_scr[s, xb_rows(c * chunk - 1 + k, chunk), :]
            g = _dot(xs.astype(BF16), wg_ref[s]) + bg_ref[s]
            xh = 0.5 * xs
            for d, (a_scr, b_scr) in enumerate(((af_scr, bf_scr), (ab_scr, bb_scr))):
                tr = jnp.tanh(g[:, d * LANES:(d + 1) * LANES])
                ti = jnp.tanh(g[:, (2 + d) * LANES:(3 + d) * LANES])
                a = jnp.exp2(c1[d] * tr + c1[d])
                a_scr[s, rows, :] = a
                v = 1.0 - a * a
                root = jnp.where(v > 0.0, v * lax.rsqrt(v), 0.0)
                b_scr[s, rows, :] = root * (ti * xh + xh)

    def seg_rows(j):
        return pl.ds(j, SUBLANES, stride=seg)

    def run_steps(step, init):
        unroll = 8
        main = seg // unroll

        def body(i, carry):
            for u in range(unroll):
                carry = step(i * unroll + u, carry)
            return carry

        carry = lax.fori_loop(0, main, body, init)
        for j in range(main * unroll, seg):
            carry = step(j, carry)
        return carry

    def pass1(j, carry):
        jb = seg - 1 - j
        out = []
        for s in range(n_slab):
            pf, hf, pb, hb = carry[s]
            a = af_scr[s, seg_rows(j), :]
            b = bf_scr[s, seg_rows(j), :]
            a2 = ab_scr[s, seg_rows(jb), :]
            b2 = bb_scr[s, seg_rows(jb), :]
            out.append((pf * a, a * hf + b, pb * a2, a2 * hb + b2))
        return tuple(out)

    one = jnp.ones((SUBLANES, LANES), F32)
    zero = jnp.zeros((SUBLANES, LANES), F32)
    totals = run_steps(pass1, tuple((one, zero, one, zero) for _ in range(n_slab)))

    rid = lax.broadcasted_iota(jnp.int32, (SUBLANES, LANES), 0)
    init_f, init_b = [], []
    for s in range(n_slab):
        lanes = slice(s * LANES, (s + 1) * LANES)
        pf, hf, pb, hb = totals[s]
        st = h0_ref[0, 0:1, lanes]
        ini = zero
        for k in range(SUBLANES):
            ini = jnp.where(rid == k, st, ini)
            st = hf[k:k + 1] + pf[k:k + 1] * st
        init_f.append(ini)
        fin_ref[0, 0:1, lanes] = st
        st = h0_ref[0, 1:2, lanes]
        ini = zero
        for k in range(SUBLANES - 1, -1, -1):
            ini = jnp.where(rid == k, st, ini)
            st = hb[k:k + 1] + pb[k:k + 1] * st
        init_b.append(ini)
        fin_ref[0, 1:2, lanes] = st

    def pass2(j, carry):
        jb = seg - 1 - j
        out = []
        for s in range(n_slab):
            hf, hb = carry[s]
            hf = af_scr[s, seg_rows(j), :] * hf + bf_scr[s, seg_rows(j), :]
            hf_scr[s, seg_rows(j), :] = hf
            hb = ab_scr[s, seg_rows(jb), :] * hb + bb_scr[s, seg_rows(jb), :]
            hb_scr[s, seg_rows(jb), :] = hb
            out.append((hf, hb))
        return tuple(out)

    run_steps(pass2, tuple(zip(init_f, init_b)))

    for c in range(n_chunk):
        rows = slice(c * chunk, (c + 1) * chunk)
        for s in range(n_slab):
            lanes = slice(s * LANES, (s + 1) * LANES)
            y_ref[0, rows, lanes] = ((hf_scr[s, rows, :] + hb_scr[s, rows, :]) * yb_scr[rows, lanes]).astype(BF16)


def _odd_segment_length(seq):
    seg = -(-seq // SUBLANES)
    return seg if seg % 2 == 1 else seg + 1


def _rg_call(x, mod, layer, row_of_batch, gn, w_in, w_index, conv_w, conv_b, wg, bg, lam, h0, *, cblk):
    b, seq, d = x.shape
    d_rnn = lam.shape[1]
    ncb = d_rnn // cblk
    n_slab = cblk // LANES
    seg = _odd_segment_length(seq)
    seq_pad = SUBLANES * seg
    chunk = min(seq, 512)
    slab = pltpu.VMEM((n_slab, seq_pad, LANES), F32)
    return pl.pallas_call(
        functools.partial(_rg_kernel, seq=seq, seg=seg, chunk=chunk),
        grid=(b, ncb),
        in_specs=[
            pl.BlockSpec((1, seq, d), lambda i, c: (i, 0, 0)),
            pl.BlockSpec((1, 1, 6, d), lambda i, c: (layer, row_of_batch(i), 0, 0)),
            pl.BlockSpec((1, d), lambda i, c: (0, 0)),
            pl.BlockSpec((None, d, cblk), lambda i, c: (w_index, 0, c)),
            pl.BlockSpec((None, d, cblk), lambda i, c: (w_index, 0, ncb + c)),
            pl.BlockSpec((4, cblk), lambda i, c: (0, c)),
            pl.BlockSpec((1, cblk), lambda i, c: (0, c)),
            pl.BlockSpec((n_slab, RG_BS, 4 * RG_BS), lambda i, c: (c, 0, 0)),
            pl.BlockSpec((n_slab, 1, 4 * RG_BS), lambda i, c: (c, 0, 0)),
            pl.BlockSpec((2, cblk), lambda i, c: (0, c)),
            pl.BlockSpec((1, 2, cblk), lambda i, c: (i, 0, c)),
        ],
        out_specs=[
            pl.BlockSpec((1, seq, cblk), lambda i, c: (i, 0, c)),
            pl.BlockSpec((1, 2, cblk), lambda i, c: (i, 0, c)),
        ],
        out_shape=[
            jax.ShapeDtypeStruct((b, seq, d_rnn), BF16),
            jax.ShapeDtypeStruct((b, 2, d_rnn), F32),
        ],
        scratch_shapes=[
            pltpu.VMEM((seq, d), BF16),
            pltpu.VMEM((n_slab, 2 * (seq + 2 * SUBLANES), LANES), F32),
            pltpu.VMEM((seq, cblk), F32),
            slab, slab, slab, slab,
            slab, slab,
        ],
        compiler_params=pltpu.CompilerParams(
            dimension_semantics=("arbitrary", "arbitrary"), vmem_limit_bytes=VMEM_LIMIT),
        name="rg_core",
    )(x, mod, gn, w_in, w_in, conv_w, conv_b, wg, bg, lam, h0)


def _rope_pairs(v, tab):
    p = v * tab
    return p + pltpu.roll(p, QK_ROPE, axis=1)


def _mlaproj_kernel(x_ref, mod_ref, gn_ref, wd_ref, gq_ref, gkv_ref, wuq_ref, wk_ref, wv_ref, tab_ref,
                    tabt_ref, q_ref, k_ref, v_ref, *cache_refs, q_scale):
    mod = mod_ref[0, 0]
    h = _modnorm(x_ref[...], gn_ref[...], mod[0:1], mod[1:2]).astype(BF16)
    proj = _dot(h, wd_ref[...])
    cq = _rms(proj[:, :Q_LORA], gq_ref[...])
    ckv = _rms(proj[:, Q_LORA:Q_LORA + KV_LORA], gkv_ref[...])
    kp2 = proj[:, Q_LORA + KV_LORA:]
    if cache_refs:
        ckv_ref, kpe_ref = cache_refs
        ckv_ref[...] = ckv
        kpe_ref[...] = kp2[:, :QK_ROPE]
    tab = tab_ref[...]
    lane = lax.broadcasted_iota(jnp.int32, kp2.shape, 1)
    kr = jnp.where(lane < QK_ROPE, _rope_pairs(kp2, tab), 0.0).astype(BF16)
    qt = _dot_nt(wuq_ref[...], cq.astype(BF16))
    ckvb = ckv.astype(BF16)
    kn = _dot(ckvb, wk_ref[...])
    v_ref[0] = _dot_nt(wv_ref[...], ckvb).astype(BF16)
    tabt = tabt_ref[...] * q_scale
    for hd in range(N_HEADS):
        lo = hd * HEAD_SLOT
        q_ref[0, lo:lo + QK_NOPE, :] = (qt[lo:lo + QK_NOPE] * q_scale).astype(BF16)
        pr = qt[lo + QK_NOPE:lo + HEAD_SLOT] * tabt
        qr = (pr[:QK_ROPE] + pr[QK_ROPE:]).astype(BF16)
        q_ref[0, lo + QK_NOPE:lo + QK_NOPE + QK_ROPE, :] = qr
        q_ref[0, lo + QK_NOPE + QK_ROPE:lo + HEAD_SLOT, :] = qr
        k_ref[:, lo:lo + QK_NOPE] = kn[:, hd * QK_NOPE:(hd + 1) * QK_NOPE].astype(BF16)
        k_ref[:, lo + QK_NOPE:lo + HEAD_SLOT] = kr


def _mlaproj_call(x, mod, layer, row_of_tile, gn, wd, gq, gkv, wuq_t, wk, wv_t, tab, tab_t, tab_tile, *,
                  seq, tm, emit_cache):
    ntok, d = x.shape
    nt = seq // tm
    const = dict(pipeline_mode=pl.Buffered(1))
    out_specs = [
        pl.BlockSpec((1, N_HEADS * HEAD_SLOT, tm), lambda i: (i // nt, 0, i % nt)),
        pl.BlockSpec((tm, N_HEADS * HEAD_SLOT), lambda i: (i, 0)),
        pl.BlockSpec((1, N_HEADS * V_DIM, tm), lambda i: (i // nt, 0, i % nt)),
    ]
    out_shape = [
        jax.ShapeDtypeStruct((ntok // seq, N_HEADS * HEAD_SLOT, seq), BF16),
        jax.ShapeDtypeStruct((ntok, N_HEADS * HEAD_SLOT), BF16),
        jax.ShapeDtypeStruct((ntok // seq, N_HEADS * V_DIM, seq), BF16),
    ]
    if emit_cache:
        out_specs += [pl.BlockSpec((tm, KV_LORA), lambda i: (i, 0)),
                      pl.BlockSpec((tm, QK_ROPE), lambda i: (i, 0))]
        out_shape += [jax.ShapeDtypeStruct((ntok, KV_LORA), F32),
                      jax.ShapeDtypeStruct((ntok, QK_ROPE), F32)]
    return pl.pallas_call(
        functools.partial(_mlaproj_kernel, q_scale=float((QK_NOPE + QK_ROPE) ** -0.5 * LOG2_E)),
        grid=(ntok // tm,),
        in_specs=[
            pl.BlockSpec((tm, d), lambda i: (i, 0)),
            pl.BlockSpec((1, 1, 6, d), lambda i: (layer, row_of_tile(i), 0, 0)),
            pl.BlockSpec((1, d), lambda i: (0, 0)),
            pl.BlockSpec(wd.shape, lambda i: (0, 0), **const),
            pl.BlockSpec((1, Q_LORA), lambda i: (0, 0)),
            pl.BlockSpec((1, KV_LORA), lambda i: (0, 0)),
            pl.BlockSpec(wuq_t.shape, lambda i: (0, 0), **const),
            pl.BlockSpec(wk.shape, lambda i: (0, 0), **const),
            pl.BlockSpec(wv_t.shape, lambda i: (0, 0), **const),
            pl.BlockSpec((tm, LANES), lambda i: (tab_tile(i), 0)),
            pl.BlockSpec((LANES, tm), lambda i: (0, tab_tile(i))),
        ],
        out_specs=out_specs,
        out_shape=out_shape,
        compiler_params=pltpu.CompilerParams(
            dimension_semantics=("arbitrary",), vmem_limit_bytes=VMEM_LIMIT),
        name="mla_proj",
    )(x, mod, gn, wd, gq, gkv, wuq_t, wk, wv_t, tab, tab_t)


def _cachekv_kernel(ckv_ref, kpe_ref, wk_ref, wv_ref, k_ref, v_ref):
    ckvb = ckv_ref[...].astype(BF16)
    kn = _dot(ckvb, wk_ref[...])
    v_ref[0] = _dot_nt(wv_ref[...], ckvb).astype(BF16)
    kp = kpe_ref[...].astype(BF16)
    for hd in range(N_HEADS):
        lo = hd * HEAD_SLOT
        k_ref[:, lo:lo + QK_NOPE] = kn[:, hd * QK_NOPE:(hd + 1) * QK_NOPE].astype(BF16)
        k_ref[:, lo + QK_NOPE:lo + HEAD_SLOT] = kp


def _cachekv_call(ckv, kpe_padded, wk, wv_t, *, past):
    ntok = ckv.shape[0]
    return pl.pallas_call(
        _cachekv_kernel,
        grid=(ntok // past,),
        in_specs=[
            pl.BlockSpec((past, KV_LORA), lambda i: (i, 0)),
            pl.BlockSpec((past, LANES), lambda i: (i, 0)),
            pl.BlockSpec(wk.shape, lambda i: (0, 0)),
            pl.BlockSpec(wv_t.shape, lambda i: (0, 0)),
        ],
        out_specs=[
            pl.BlockSpec((past, N_HEADS * HEAD_SLOT), lambda i: (i, 0)),
            pl.BlockSpec((1, N_HEADS * V_DIM, past), lambda i: (i, 0, 0)),
        ],
        out_shape=[
            jax.ShapeDtypeStruct((ntok, N_HEADS * HEAD_SLOT), BF16),
            jax.ShapeDtypeStruct((ntok // past, N_HEADS * V_DIM, past), BF16),
        ],
        compiler_params=pltpu.CompilerParams(
            dimension_semantics=("arbitrary",), vmem_limit_bytes=VMEM_LIMIT),
        name="mla_cache_kv",
    )(ckv, kpe_padded, wk, wv_t)


def _attn_kernel(q_ref, k_ref, v_ref, *rest):
    if len(rest) == 4:
        kc_ref, vc_ref, o_ref, s_scr = rest
    else:
        kc_ref = vc_ref = None
        o_ref, s_scr = rest
    kchunk = 256
    chunks = [(k_ref, v_ref, r, r) for r in range(0, k_ref.shape[1], kchunk)]
    if kc_ref is not None:
        chunks += [(kc_ref, vc_ref, r, k_ref.shape[1] + r) for r in range(0, kc_ref.shape[1], kchunk)]

    tq = s_scr.shape[2]
    n_item = (q_ref.shape[2] // tq) * N_HEADS

    def score_chunk(it, chunk):
        kref, _, r, srow = chunk
        sub, hd = divmod(it, N_HEADS)
        slot = slice(hd * HEAD_SLOT, (hd + 1) * HEAD_SLOT)
        s = _dot(kref[0, r:r + kchunk, slot], q_ref[0, slot, sub * tq:(sub + 1) * tq])
        s_scr[it % n_buf, srow:srow + kchunk, :] = s
        return jnp.max(s, axis=0, keepdims=True)

    def value_chunk(it, chunk, m):
        _, vref, r, srow = chunk
        hd = it % N_HEADS
        p = jnp.exp2(s_scr[it % n_buf, srow:srow + kchunk, :] - m)
        ov = _dot(vref[0, hd * V_DIM:(hd + 1) * V_DIM, r:r + kchunk], p.astype(BF16))
        return jnp.sum(p, axis=0, keepdims=True), ov

    def running_max(acc, new):
        return new if acc is None else jnp.maximum(acc, new)

    n_buf = s_scr.shape[0]
    nc = len(chunks)
    ahead = nc + min(nc, 6)
    assert n_buf == 3 and nc < ahead <= 2 * nc
    total = n_item * nc
    maxes = {}

    def issue_scores(i):
        it, c = divmod(i, nc)
        maxes[it] = running_max(maxes.get(it), score_chunk(it, chunks[c]))

    for i in range(ahead):
        issue_scores(i)
    l, ov = None, None
    for j in range(total):
        if j + ahead < total:
            issue_scores(j + ahead)
        it, c = divmod(j, nc)
        l_c, ov_c = value_chunk(it, chunks[c], maxes[it])
        l = l_c if c == 0 else l + l_c
        ov = ov_c if c == 0 else ov + ov_c
        if c == nc - 1:
            sub, hd = divmod(it, N_HEADS)
            o_ref[sub * tq:(sub + 1) * tq, hd * V_DIM:(hd + 1) * V_DIM] = (ov * (1.0 / l)).T.astype(BF16)


def _attn_call(q_t, k, v_t, kc, vc_t, *, tq, nsub):
    batch, kw, seq = q_t.shape
    ntok = batch * seq
    tstep = tq * nsub
    nq = seq // tstep
    vw = N_HEADS * V_DIM
    args = [q_t, k.reshape(batch, seq, kw), v_t]
    in_specs = [
        pl.BlockSpec((1, kw, tstep), lambda b, i: (b, 0, i)),
        pl.BlockSpec((1, seq, kw), lambda b, i: (b, 0, 0)),
        pl.BlockSpec((1, vw, seq), lambda b, i: (b, 0, 0)),
    ]
    past = 0
    if kc is not None:
        past = kc.shape[0] // batch
        args += [kc.reshape(batch, past, kw), vc_t]
        in_specs += [pl.BlockSpec((1, past, kw), lambda b, i: (b, 0, 0)),
                     pl.BlockSpec((1, vw, past), lambda b, i: (b, 0, 0))]
    return pl.pallas_call(
        _attn_kernel,
        grid=(batch, nq),
        in_specs=in_specs,
        out_specs=pl.BlockSpec((tstep, vw), lambda b, i: (b * nq + i, 0)),
        out_shape=jax.ShapeDtypeStruct((ntok, vw), BF16),
        scratch_shapes=[pltpu.VMEM((3, seq + past, tq), F32)],
        compiler_params=pltpu.CompilerParams(
            dimension_semantics=("arbitrary", "arbitrary"), vmem_limit_bytes=VMEM_LIMIT),
        name="mla_attn",
    )(*args)


def _pair_partner(w):
    k, n = w.shape
    wp = w.reshape(k, n // 2, 2)
    return jnp.stack([-wp[..., 1], wp[..., 0]], axis=-1).reshape(k, n)


def _rope_table(n):
    rows = n // GRID_W
    row = jnp.repeat(jnp.arange(rows, dtype=F32), GRID_W)
    col = jnp.tile(jnp.arange(GRID_W, dtype=F32), rows)
    half = QK_ROPE // 2
    inv = ROPE_THETA ** (-jnp.arange(0, half, 2, dtype=F32) / half)
    ang = jnp.concatenate([row[:, None] * inv, col[:, None] * inv], axis=-1)
    ang = jnp.repeat(ang, 2, axis=-1)
    return jnp.concatenate([jnp.cos(ang), jnp.sin(ang)], axis=-1)


def _identity_rope_table(n):
    return jnp.concatenate([jnp.ones((n, QK_ROPE), F32), jnp.zeros((n, QK_ROPE), F32)], axis=-1)


def kernel(x_prompt, x_sample, state_rglru, cache_ckv, cache_kpe, c, c_ctx, ada_w, ada_b, norm_mix, norm_mlp, mlp_w1, mlp_w2, rg_w_in, rg_conv_w, rg_conv_b, rg_wa, rg_ba, rg_wx, rg_bx, rg_lambda, rg_w_out, mla_w_dqkv, mla_norm_q, mla_norm_kv, mla_w_uq, mla_w_ukv, mla_w_o, final_norm):
    batch, seq, d = x_prompt.shape
    dec_batch, dec_seq, _ = x_sample.shape
    depth = ada_w.shape[0]
    d_rnn = rg_lambda.shape[-1]
    past = cache_ckv.shape[2]
    assert dec_batch == COND_ROWS - SAMPLE_ROW0

    cond = jnp.concatenate(
        [jnp.broadcast_to(c_ctx[None, :], (SAMPLE_ROW0, d)), c], axis=0)
    mod = _ada_call(cond, ada_w, ada_b).reshape(depth, COND_ROWS, 6, d)

    w1 = mlp_w1.astype(BF16)
    w2 = mlp_w2.astype(BF16)
    w_in = rg_w_in.astype(BF16)
    w_out = rg_w_out.astype(BF16)
    w_o = mla_w_o.astype(BF16)
    n_rg, n_mla = rg_w_in.shape[0], mla_w_dqkv.shape[0]
    nblk = d_rnn // RG_BS
    wg = (0.5 * jnp.concatenate([rg_wa[:, 0], rg_wa[:, 1], rg_wx[:, 0], rg_wx[:, 1]], axis=-1)).astype(BF16)
    bg = 0.5 * jnp.concatenate(
        [rg_ba[:, 0].reshape(n_rg, nblk, 1, RG_BS), rg_ba[:, 1].reshape(n_rg, nblk, 1, RG_BS),
         rg_bx[:, 0].reshape(n_rg, nblk, 1, RG_BS), rg_bx[:, 1].reshape(n_rg, nblk, 1, RG_BS)], axis=-1)
    lo = Q_LORA + KV_LORA
    wd, wuq, wk, wv = [], [], [], []
    for j in range(n_mla):
        w = mla_w_dqkv[j]
        wd.append(jnp.concatenate([w, _pair_partner(w[:, lo:])], axis=1).astype(BF16))
        wq = mla_w_uq[j].reshape(Q_LORA, N_HEADS, QK_NOPE + QK_ROPE)
        wq_pe = wq[..., QK_NOPE:]
        wq_pp = jnp.stack([_pair_partner(wq_pe[:, hd]) for hd in range(N_HEADS)], axis=1)
        wuq.append(jnp.concatenate([wq, wq_pp], axis=-1).reshape(Q_LORA, N_HEADS * HEAD_SLOT).T.astype(BF16))
        wkv = mla_w_ukv[j].reshape(KV_LORA, N_HEADS, QK_NOPE + V_DIM)
        wk.append(wkv[..., :QK_NOPE].reshape(KV_LORA, N_HEADS * QK_NOPE).astype(BF16))
        wv.append(wkv[..., QK_NOPE:].reshape(KV_LORA, N_HEADS * V_DIM).T.astype(BF16))

    fn = final_norm.reshape(1, d)
    tab_sample = _rope_table(dec_seq)
    tm = 512

    def run_stream(x, h0_all, cache, *, sample):
        b, t, _ = x.shape
        tmp = min(tm, t)
        if sample:
            row_of_batch = lambda i: SAMPLE_ROW0 + i
            row_of_tile = lambda i: SAMPLE_ROW0 + i // (t // tm)
            row_of_ptile = lambda i: SAMPLE_ROW0 + i // (t // tmp)
            tab, tab_tile = tab_sample, (lambda i: i % (t // tmp))
        else:
            row_of_batch = lambda i: 0
            row_of_tile = lambda i: 0
            row_of_ptile = lambda i: 0
            tab, tab_tile = _identity_rope_table(tmp), (lambda i: 0)
        tab_t = tab.T
        xt = x.reshape(b * t, d)
        states, ckvs, kpes = [], [], []
        for l in range(depth):
            j = l // 2
            if l % 2 == 0:
                y, fin = _rg_call(xt.reshape(b, t, d), mod, l, row_of_batch, norm_mix[l].reshape(1, d),
                                  w_in, j, rg_conv_w[j], rg_conv_b[j].reshape(1, d_rnn), wg[j], bg[j],
                                  rg_lambda[j], h0_all[:, j], cblk=256)
                states.append(fin)
                y = y.reshape(b * t, d_rnn)
                wp = w_out
            else:
                outs = _mlaproj_call(xt, mod, l, row_of_ptile, norm_mix[l].reshape(1, d), wd[j],
                                     mla_norm_q[j].reshape(1, Q_LORA), mla_norm_kv[j].reshape(1, KV_LORA),
                                     wuq[j], wk[j], wv[j], tab, tab_t, tab_tile, seq=t, tm=tmp,
                                     emit_cache=not sample)
                q, k, v = outs[:3]
                if sample:
                    ckv_c, kpe_c = cache
                    kpe_p = jnp.pad(kpe_c[:, j].reshape(b * past, QK_ROPE), ((0, 0), (0, LANES - QK_ROPE)))
                    kc, vc = _cachekv_call(ckv_c[:, j].reshape(b * past, KV_LORA), kpe_p, wk[j], wv[j], past=past)
                else:
                    kc = vc = None
                    ckvs.append(outs[3].reshape(b, t, KV_LORA))
                    kpes.append(outs[4].reshape(b, t, QK_ROPE))
                y = _attn_call(q, k, v, kc, vc, tq=min(512, t), nsub=1)
                wp = w_o
            xt = _post_call(xt, y, mod, l, row_of_tile, norm_mlp[l].reshape(1, d), wp, j, w1, w2, fn,
                            tm=tm, final=(l == depth - 1))
        return xt.reshape(b, t, d), states, ckvs, kpes

    zeros_h0 = jnp.zeros((batch, n_rg, 2, d_rnn), F32)
    y_prompt, states, ckvs, kpes = run_stream(x_prompt, zeros_h0, None, sample=False)
    y_sample, _, _, _ = run_stream(x_sample, state_rglru, (cache_ckv, cache_kpe), sample=True)
    return (y_prompt, y_sample, jnp.stack(states, axis=1), jnp.stack(ckvs, axis=1), jnp.stack(kpes, axis=1))
```

```python
import functools

import jax
import jax.numpy as jnp
from jax import lax
from jax.experimental import pallas as pl
from jax.experimental.pallas import tpu as pltpu

F32 = jnp.float32
BF16 = jnp.bfloat16

EPS = 1e-6
LOG2_E = 1.4426950408889634
RG_C = 8.0
RG_BS = 128
N_HEADS = 8
QK_NOPE = 128
QK_ROPE = 64
V_DIM = 128
Q_LORA = 512
KV_LORA = 256
GRID_W = 64
ROPE_THETA = 10000.0
HEAD_SLOT = 256

LANES = 128
SUBLANES = 8
VMEM_LIMIT = 56 * 1024 * 1024

COND_ROWS = 16
SAMPLE_ROW0 = 8


def _dot(a, b):
    return jnp.dot(a, b, preferred_element_type=F32)


def _dot_nt(a, b):
    return lax.dot_general(a, b, (((1,), (1,)), ((), ())), preferred_element_type=F32)


def _rms(x, g):
    ms = jnp.mean(x * x, axis=-1, keepdims=True)
    return x * lax.rsqrt(ms + EPS) * g


def _modnorm(x, g, shift, scale):
    ms = jnp.mean(x * x, axis=-1, keepdims=True)
    return x * lax.rsqrt(ms + EPS) * (g * (1.0 + scale)) + shift


def _sigmoid(x):
    return 0.5 * jnp.tanh(0.5 * x) + 0.5


def _gelu_tanh_of_half(xh):
    c = 0.7978845608028654
    return xh + xh * jnp.tanh(xh * (2.0 * c + (8.0 * c * 0.044715) * (xh * xh)))


def _ada_kernel(cond_ref, w_ref, b_ref, o_ref):
    c = cond_ref[...]
    s = c * _sigmoid(c)
    rows = s.shape[0]
    s_hi = s.astype(BF16)
    s_lo = (s - s_hi.astype(F32)).astype(BF16)
    w = w_ref[0]
    w_hi = w.astype(BF16)
    w_lo = (w - w_hi.astype(F32)).astype(BF16)
    main = _dot(jnp.concatenate([s_hi, s_lo], axis=0), w_hi)
    o_ref[0] = main[:rows] + main[rows:] + _dot(s_hi, w_lo) + b_ref[0]


def _ada_call(cond, ada_w, ada_b):
    depth, d, n6 = ada_w.shape
    tn = 1536
    return pl.pallas_call(
        _ada_kernel,
        grid=(depth, n6 // tn),
        in_specs=[
            pl.BlockSpec((COND_ROWS, d), lambda l, n: (0, 0)),
            pl.BlockSpec((1, d, tn), lambda l, n: (l, 0, n)),
            pl.BlockSpec((1, 1, tn), lambda l, n: (l, 0, n)),
        ],
        out_specs=pl.BlockSpec((1, COND_ROWS, tn), lambda l, n: (l, 0, n)),
        out_shape=jax.ShapeDtypeStruct((depth, COND_ROWS, n6), F32),
        compiler_params=pltpu.CompilerParams(
            dimension_semantics=("arbitrary", "arbitrary"), vmem_limit_bytes=VMEM_LIMIT),
        name="ada",
    )(cond, ada_w, ada_b.reshape(depth, 1, n6))


def _post_kernel(x_ref, y_ref, mod_ref, gn_ref, wp_ref, w1_ref, w2_ref, fn_ref, o_ref, *,
                 ff_chunk, final):
    mod = mod_ref[0, 0]
    g1, sh2, sc2, g2 = mod[2:3], mod[3:4], mod[4:5], mod[5:6]
    x1 = x_ref[...] + g1 * _dot(y_ref[...], wp_ref[...])
    h = _modnorm(x1, gn_ref[...], sh2, sc2).astype(BF16)
    ff = w1_ref.shape[1]
    acc = jnp.zeros_like(x1)
    for c in range(ff // ff_chunk):
        a = jnp.maximum(_dot(h, w1_ref[:, c * ff_chunk:(c + 1) * ff_chunk]), 0.0)
        acc = acc + _dot((a * a).astype(BF16), w2_ref[c * ff_chunk:(c + 1) * ff_chunk, :])
    out = x1 + g2 * acc
    if final:
        out = _rms(out, fn_ref[...])
    o_ref[...] = out


def _post_call(x, y, mod, layer, row_of_tile, gn, wp, wp_index, w1, w2, fn, *, tm, final):
    ntok, d = x.shape
    ff = w1.shape[2]
    const = dict(pipeline_mode=pl.Buffered(1))
    return pl.pallas_call(
        functools.partial(_post_kernel, ff_chunk=1024, final=final),
        grid=(ntok // tm,),
        in_specs=[
            pl.BlockSpec((tm, d), lambda i: (i, 0)),
            pl.BlockSpec((tm, d), lambda i: (i, 0)),
            pl.BlockSpec((1, 1, 6, d), lambda i: (layer, row_of_tile(i), 0, 0)),
            pl.BlockSpec((1, d), lambda i: (0, 0)),
            pl.BlockSpec((None, d, d), lambda i: (wp_index, 0, 0), **const),
            pl.BlockSpec((None, d, ff), lambda i: (layer, 0, 0), **const),
            pl.BlockSpec((None, ff, d), lambda i: (layer, 0, 0), **const),
            pl.BlockSpec((1, d), lambda i: (0, 0)),
        ],
        out_specs=pl.BlockSpec((tm, d), lambda i: (i, 0)),
        out_shape=jax.ShapeDtypeStruct((ntok, d), F32),
        compiler_params=pltpu.CompilerParams(
            dimension_semantics=("arbitrary",), vmem_limit_bytes=VMEM_LIMIT),
        name="post",
    )(x, y, mod, gn, wp, w1, w2, fn)


def _rg_kernel(x_ref, mod_ref, gn_ref, wx_ref, wy_ref, cw_ref, cb_ref, wg_ref, lam_ref,
               h0_ref, y_ref, fin_ref,
               h_scr, xb_scr, yb_scr, af_scr, bf_scr, ab_scr, bb_scr, hf_scr, hb_scr, *,
               seq, seg, chunk):
    n_slab = wx_ref.shape[1] // LANES
    n_chunk = seq // chunk
    pad = SUBLANES * seg - seq
    halo = SUBLANES

    @pl.when(pl.program_id(1) == 0)
    def _():
        mod = mod_ref[0, 0]
        for c in range(n_chunk):
            rows = slice(c * chunk, (c + 1) * chunk)
            h_scr[rows, :] = _modnorm(x_ref[0, rows, :], gn_ref[...], mod[0:1], mod[1:2]).astype(BF16)

    def xb_rows(t0, n):
        return pl.ds(2 * (halo + t0), n, stride=2)

    for s in range(n_slab):
        xb_scr[s, 0:2 * halo, :] = jnp.zeros((2 * halo, LANES), F32)
        xb_scr[s, 2 * (halo + seq):2 * (2 * halo + seq), :] = jnp.zeros((2 * halo, LANES), F32)
    def project(c):
        rows = slice(c * chunk, (c + 1) * chunk)
        hc = h_scr[rows, :]
        ux = _dot(hc, wx_ref[...])
        for s in range(n_slab):
            xb_scr[s, xb_rows(c * chunk, chunk), :] = ux[:, s * LANES:(s + 1) * LANES]
        yb_scr[rows, :] = _gelu_tanh_of_half(_dot(hc, wy_ref[...]))

    for s in range(n_slab):
        tail = slice(seq, seq + pad)
        ones = jnp.ones((pad, LANES), F32)
        zeros = jnp.zeros((pad, LANES), F32)
        af_scr[s, tail, :] = ones
        ab_scr[s, tail, :] = ones
        bf_scr[s, tail, :] = zeros
        bb_scr[s, tail, :] = zeros

    cw = cw_ref[...]
    lam = lam_ref[...]
    c1_all = (-0.5 * RG_C * LOG2_E) * (jnp.maximum(-lam, 0.0) + jnp.log1p(jnp.exp(-jnp.abs(lam))))
    lane = lax.broadcasted_iota(jnp.int32, (chunk, LANES), 1)
    bias_cols = jnp.where(lane < 2, 1.0, 0.0).astype(BF16)

    def gates(c):
        rows = slice(c * chunk, (c + 1) * chunk)
        for s in range(n_slab):
            lanes = slice(s * LANES, (s + 1) * LANES)
            xs = cb_ref[:, lanes] + cw[0:1, lanes] * xb_scr[s, xb_rows(c * chunk - 1, chunk), :]
            for k in range(1, 4):
                xs = xs + cw[k:k + 1, lanes] * xb_scr[s, xb_rows(c * chunk - 1 + k, chunk), :]
            g = _dot(jnp.concatenate([xs.astype(BF16), bias_cols], axis=1), wg_ref[s])
            xh = 0.5 * xs
            for d, (a_scr, b_scr) in enumerate(((af_scr, bf_scr), (ab_scr, bb_scr))):
                c1 = c1_all[d:d + 1, lanes]
                tr = jnp.tanh(g[:, d * LANES:(d + 1) * LANES])
                ti = jnp.tanh(g[:, (2 + d) * LANES:(3 + d) * LANES])
                a = jnp.exp2(c1 * tr + c1)
                a_scr[s, rows, :] = a
                v = 1.0 - a * a
                root = jnp.where(v > 0.0, v * lax.rsqrt(v), 0.0)
                b_scr[s, rows, :] = root * (ti * xh + xh)

    project(0)
    for c in range(1, n_chunk):
        project(c)
        gates(c - 1)
    gates(n_chunk - 1)

    def seg_rows(j):
        return pl.ds(j, SUBLANES, stride=seg)

    def run_steps(step, init):
        unroll = 8
        main = seg // unroll

        def body(i, carry):
            for u in range(unroll):
                carry = step(i * unroll + u, carry)
            return carry

        carry = lax.fori_loop(0, main, body, init)
        for j in range(main * unroll, seg):
            carry = step(j, carry)
        return carry

    def pass1(j, carry):
        jb = seg - 1 - j
        out = []
        for s in range(n_slab):
            pf, hf, pb, hb = carry[s]
            a = af_scr[s, seg_rows(j), :]
            b = bf_scr[s, seg_rows(j), :]
            a2 = ab_scr[s, seg_rows(jb), :]
            b2 = bb_scr[s, seg_rows(jb), :]
            out.append((pf * a, a * hf + b, pb * a2, a2 * hb + b2))
        return tuple(out)

    one = jnp.ones((SUBLANES, LANES), F32)
    zero = jnp.zeros((SUBLANES, LANES), F32)
    totals = run_steps(pass1, tuple((one, zero, one, zero) for _ in range(n_slab)))

    rid = lax.broadcasted_iota(jnp.int32, (SUBLANES, LANES), 0)
    init_f, init_b = [], []
    for s in range(n_slab):
        lanes = slice(s * LANES, (s + 1) * LANES)
        pf, hf, pb, hb = totals[s]
        st = h0_ref[0, 0:1, lanes]
        ini = zero
        for k in range(SUBLANES):
            ini = jnp.where(rid == k, st, ini)
            st = hf[k:k + 1] + pf[k:k + 1] * st
        init_f.append(ini)
        fin_ref[0, 0:1, lanes] = st
        st = h0_ref[0, 1:2, lanes]
        ini = zero
        for k in range(SUBLANES - 1, -1, -1):
            ini = jnp.where(rid == k, st, ini)
            st = hb[k:k + 1] + pb[k:k + 1] * st
        init_b.append(ini)
        fin_ref[0, 1:2, lanes] = st

    def pass2(j, carry):
        jb = seg - 1 - j
        out = []
        for s in range(n_slab):
            hf, hb = carry[s]
            hf = af_scr[s, seg_rows(j), :] * hf + bf_scr[s, seg_rows(j), :]
            hf_scr[s, seg_rows(j), :] = hf
            hb = ab_scr[s, seg_rows(jb), :] * hb + bb_scr[s, seg_rows(jb), :]
            hb_scr[s, seg_rows(jb), :] = hb
            out.append((hf, hb))
        return tuple(out)

    run_steps(pass2, tuple(zip(init_f, init_b)))

    for c in range(n_chunk):
        rows = slice(c * chunk, (c + 1) * chunk)
        for s in range(n_slab):
            lanes = slice(s * LANES, (s + 1) * LANES)
            y_ref[0, rows, lanes] = ((hf_scr[s, rows, :] + hb_scr[s, rows, :]) * yb_scr[rows, lanes]).astype(BF16)


def _odd_segment_length(seq):
    seg = -(-seq // SUBLANES)
    return seg if seg % 2 == 1 else seg + 1


def _rg_call(x, mod, layer, row_of_batch, gn, w_in, w_index, conv_w, conv_b, wg, lam, h0, *, cblk):
    b, seq, d = x.shape
    d_rnn = lam.shape[1]
    ncb = d_rnn // cblk
    n_slab = cblk // LANES
    seg = _odd_segment_length(seq)
    seq_pad = SUBLANES * seg
    chunk = min(seq, 512)
    slab = pltpu.VMEM((n_slab, seq_pad, LANES), F32)
    return pl.pallas_call(
        functools.partial(_rg_kernel, seq=seq, seg=seg, chunk=chunk),
        grid=(b, ncb),
        in_specs=[
            pl.BlockSpec((1, seq, d), lambda i, c: (i, 0, 0)),
            pl.BlockSpec((1, 1, 6, d), lambda i, c: (layer, row_of_batch(i), 0, 0)),
            pl.BlockSpec((1, d), lambda i, c: (0, 0)),
            pl.BlockSpec((None, d, cblk), lambda i, c: (w_index, 0, c)),
            pl.BlockSpec((None, d, cblk), lambda i, c: (w_index, 0, ncb + c)),
            pl.BlockSpec((4, cblk), lambda i, c: (0, c)),
            pl.BlockSpec((1, cblk), lambda i, c: (0, c)),
            pl.BlockSpec((n_slab, 2 * RG_BS, 4 * RG_BS), lambda i, c: (c, 0, 0)),
            pl.BlockSpec((2, cblk), lambda i, c: (0, c)),
            pl.BlockSpec((1, 2, cblk), lambda i, c: (i, 0, c)),
        ],
        out_specs=[
            pl.BlockSpec((1, seq, cblk), lambda i, c: (i, 0, c)),
            pl.BlockSpec((1, 2, cblk), lambda i, c: (i, 0, c)),
        ],
        out_shape=[
            jax.ShapeDtypeStruct((b, seq, d_rnn), BF16),
            jax.ShapeDtypeStruct((b, 2, d_rnn), F32),
        ],
        scratch_shapes=[
            pltpu.VMEM((seq, d), BF16),
            pltpu.VMEM((n_slab, 2 * (seq + 2 * SUBLANES), LANES), F32),
            pltpu.VMEM((seq, cblk), F32),
            slab, slab, slab, slab,
            slab, slab,
        ],
        compiler_params=pltpu.CompilerParams(
            dimension_semantics=("arbitrary", "arbitrary"), vmem_limit_bytes=VMEM_LIMIT),
        name="rg_core",
    )(x, mod, gn, w_in, w_in, conv_w, conv_b, wg, lam, h0)


def _rope_pairs(v, tab):
    p = v * tab
    return p + pltpu.roll(p, QK_ROPE, axis=1)


def _mlaproj_kernel(x_ref, mod_ref, gn_ref, wd_ref, gq_ref, gkv_ref, wuq_ref, wk_ref, wv_ref, tab_ref,
                    tabt_ref, q_ref, k_ref, v_ref, *cache_refs, q_scale):
    mod = mod_ref[0, 0]
    h = _modnorm(x_ref[...], gn_ref[...], mod[0:1], mod[1:2]).astype(BF16)
    proj = _dot(h, wd_ref[...])
    cq = _rms(proj[:, :Q_LORA], gq_ref[...])
    ckv = _rms(proj[:, Q_LORA:Q_LORA + KV_LORA], gkv_ref[...])
    kp2 = proj[:, Q_LORA + KV_LORA:]
    if cache_refs:
        ckv_ref, kpe_ref = cache_refs
        ckv_ref[...] = ckv
        kpe_ref[...] = kp2[:, :QK_ROPE]
    tab = tab_ref[...]
    lane = lax.broadcasted_iota(jnp.int32, kp2.shape, 1)
    kr = jnp.where(lane < QK_ROPE, _rope_pairs(kp2, tab), 0.0).astype(BF16)
    qt = _dot_nt(wuq_ref[...], cq.astype(BF16))
    ckvb = ckv.astype(BF16)
    kn = _dot(ckvb, wk_ref[...])
    v_ref[0] = _dot_nt(wv_ref[...], ckvb).astype(BF16)
    tabt = tabt_ref[...] * q_scale
    for hd in range(N_HEADS):
        lo = hd * HEAD_SLOT
        q_ref[0, lo:lo + QK_NOPE, :] = (qt[lo:lo + QK_NOPE] * q_scale).astype(BF16)
        pr = qt[lo + QK_NOPE:lo + HEAD_SLOT] * tabt
        qr = (pr[:QK_ROPE] + pr[QK_ROPE:]).astype(BF16)
        q_ref[0, lo + QK_NOPE:lo + QK_NOPE + QK_ROPE, :] = qr
        q_ref[0, lo + QK_NOPE + QK_ROPE:lo + HEAD_SLOT, :] = qr
        k_ref[:, lo:lo + QK_NOPE] = kn[:, hd * QK_NOPE:(hd + 1) * QK_NOPE].astype(BF16)
        k_ref[:, lo + QK_NOPE:lo + HEAD_SLOT] = kr


def _mlaproj_call(x, mod, layer, row_of_tile, gn, wd, gq, gkv, wuq_t, wk, wv_t, tab, tab_t, tab_tile, *,
                  seq, tm, emit_cache):
    ntok, d = x.shape
    nt = seq // tm
    const = dict(pipeline_mode=pl.Buffered(1))
    out_specs = [
        pl.BlockSpec((1, N_HEADS * HEAD_SLOT, tm), lambda i: (i // nt, 0, i % nt)),
        pl.BlockSpec((tm, N_HEADS * HEAD_SLOT), lambda i: (i, 0)),
        pl.BlockSpec((1, N_HEADS * V_DIM, tm), lambda i: (i // nt, 0, i % nt)),
    ]
    out_shape = [
        jax.ShapeDtypeStruct((ntok // seq, N_HEADS * HEAD_SLOT, seq), BF16),
        jax.ShapeDtypeStruct((ntok, N_HEADS * HEAD_SLOT), BF16),
        jax.ShapeDtypeStruct((ntok // seq, N_HEADS * V_DIM, seq), BF16),
    ]
    if emit_cache:
        out_specs += [pl.BlockSpec((tm, KV_LORA), lambda i: (i, 0)),
                      pl.BlockSpec((tm, QK_ROPE), lambda i: (i, 0))]
        out_shape += [jax.ShapeDtypeStruct((ntok, KV_LORA), F32),
                      jax.ShapeDtypeStruct((ntok, QK_ROPE), F32)]
    return pl.pallas_call(
        functools.partial(_mlaproj_kernel, q_scale=float((QK_NOPE + QK_ROPE) ** -0.5 * LOG2_E)),
        grid=(ntok // tm,),
        in_specs=[
            pl.BlockSpec((tm, d), lambda i: (i, 0)),
            pl.BlockSpec((1, 1, 6, d), lambda i: (layer, row_of_tile(i), 0, 0)),
            pl.BlockSpec((1, d), lambda i: (0, 0)),
            pl.BlockSpec(wd.shape, lambda i: (0, 0), **const),
            pl.BlockSpec((1, Q_LORA), lambda i: (0, 0)),
            pl.BlockSpec((1, KV_LORA), lambda i: (0, 0)),
            pl.BlockSpec(wuq_t.shape, lambda i: (0, 0), **const),
            pl.BlockSpec(wk.shape, lambda i: (0, 0), **const),
            pl.BlockSpec(wv_t.shape, lambda i: (0, 0), **const),
            pl.BlockSpec((tm, LANES), lambda i: (tab_tile(i), 0)),
            pl.BlockSpec((LANES, tm), lambda i: (0, tab_tile(i))),
        ],
        out_specs=out_specs,
        out_shape=out_shape,
        compiler_params=pltpu.CompilerParams(
            dimension_semantics=("arbitrary",), vmem_limit_bytes=VMEM_LIMIT),
        name="mla_proj",
    )(x, mod, gn, wd, gq, gkv, wuq_t, wk, wv_t, tab, tab_t)


def _cachekv_kernel(ckv_ref, kpe_ref, wk_ref, wv_ref, k_ref, v_ref):
    ckvb = ckv_ref[...].astype(BF16)
    kn = _dot(ckvb, wk_ref[...])
    v_ref[0] = _dot_nt(wv_ref[...], ckvb).astype(BF16)
    kp = kpe_ref[...].astype(BF16)
    for hd in range(N_HEADS):
        lo = hd * HEAD_SLOT
        k_ref[:, lo:lo + QK_NOPE] = kn[:, hd * QK_NOPE:(hd + 1) * QK_NOPE].astype(BF16)
        k_ref[:, lo + QK_NOPE:lo + HEAD_SLOT] = kp


def _cachekv_call(ckv, kpe_padded, wk, wv_t, *, past):
    ntok = ckv.shape[0]
    return pl.pallas_call(
        _cachekv_kernel,
        grid=(ntok // past,),
        in_specs=[
            pl.BlockSpec((past, KV_LORA), lambda i: (i, 0)),
            pl.BlockSpec((past, LANES), lambda i: (i, 0)),
            pl.BlockSpec(wk.shape, lambda i: (0, 0)),
            pl.BlockSpec(wv_t.shape, lambda i: (0, 0)),
        ],
        out_specs=[
            pl.BlockSpec((past, N_HEADS * HEAD_SLOT), lambda i: (i, 0)),
            pl.BlockSpec((1, N_HEADS * V_DIM, past), lambda i: (i, 0, 0)),
        ],
        out_shape=[
            jax.ShapeDtypeStruct((ntok, N_HEADS * HEAD_SLOT), BF16),
            jax.ShapeDtypeStruct((ntok // past, N_HEADS * V_DIM, past), BF16),
        ],
        compiler_params=pltpu.CompilerParams(
            dimension_semantics=("arbitrary",), vmem_limit_bytes=VMEM_LIMIT),
        name="mla_cache_kv",
    )(ckv, kpe_padded, wk, wv_t)


def _attn_kernel(q_ref, k_ref, v_ref, *rest):
    if len(rest) == 4:
        kc_ref, vc_ref, o_ref, s_scr = rest
    else:
        kc_ref = vc_ref = None
        o_ref, s_scr = rest
    kchunk = min(512, k_ref.shape[1])
    chunks = [(k_ref, v_ref, r, r, kchunk) for r in range(0, k_ref.shape[1], kchunk)]
    if kc_ref is not None:
        cchunk = min(kchunk, kc_ref.shape[1])
        chunks += [(kc_ref, vc_ref, r, k_ref.shape[1] + r, cchunk) for r in range(0, kc_ref.shape[1], cchunk)]

    tq = s_scr.shape[2]
    n_item = (q_ref.shape[2] // tq) * N_HEADS

    def score_chunk(it, chunk):
        kref, _, r, srow, n = chunk
        sub, hd = divmod(it, N_HEADS)
        slot = slice(hd * HEAD_SLOT, (hd + 1) * HEAD_SLOT)
        s = _dot(kref[0, r:r + n, slot], q_ref[0, slot, sub * tq:(sub + 1) * tq])
        s_scr[it % n_buf, srow:srow + n, :] = s
        return jnp.max(s, axis=0, keepdims=True)

    def value_chunk(it, chunk, m):
        _, vref, r, srow, n = chunk
        hd = it % N_HEADS
        p = jnp.exp2(s_scr[it % n_buf, srow:srow + n, :] - m)
        ov = _dot(vref[0, hd * V_DIM:(hd + 1) * V_DIM, r:r + n], p.astype(BF16))
        return jnp.sum(p, axis=0, keepdims=True), ov

    def running_max(acc, new):
        return new if acc is None else jnp.maximum(acc, new)

    n_buf = s_scr.shape[0]
    nc = len(chunks)
    ahead = nc + min(nc, 3)
    assert n_buf == 3 and nc < ahead <= 2 * nc
    total = n_item * nc
    maxes = {}

    def issue_scores(i):
        it, c = divmod(i, nc)
        maxes[it] = running_max(maxes.get(it), score_chunk(it, chunks[c]))

    for i in range(ahead):
        issue_scores(i)
    l, ov = None, None
    for j in range(total):
        if j + ahead < total:
            issue_scores(j + ahead)
        it, c = divmod(j, nc)
        l_c, ov_c = value_chunk(it, chunks[c], maxes[it])
        l = l_c if c == 0 else l + l_c
        ov = ov_c if c == 0 else ov + ov_c
        if c == nc - 1:
            sub, hd = divmod(it, N_HEADS)
            o_ref[sub * tq:(sub + 1) * tq, hd * V_DIM:(hd + 1) * V_DIM] = (ov * (1.0 / l)).T.astype(BF16)


def _attn_call(q_t, k, v_t, kc, vc_t, *, tq, nsub):
    batch, kw, seq = q_t.shape
    ntok = batch * seq
    tstep = tq * nsub
    nq = seq // tstep
    vw = N_HEADS * V_DIM
    args = [q_t, k.reshape(batch, seq, kw), v_t]
    in_specs = [
        pl.BlockSpec((1, kw, tstep), lambda b, i: (b, 0, i)),
        pl.BlockSpec((1, seq, kw), lambda b, i: (b, 0, 0)),
        pl.BlockSpec((1, vw, seq), lambda b, i: (b, 0, 0)),
    ]
    past = 0
    if kc is not None:
        past = kc.shape[0] // batch
        args += [kc.reshape(batch, past, kw), vc_t]
        in_specs += [pl.BlockSpec((1, past, kw), lambda b, i: (b, 0, 0)),
                     pl.BlockSpec((1, vw, past), lambda b, i: (b, 0, 0))]
    return pl.pallas_call(
        _attn_kernel,
        grid=(batch, nq),
        in_specs=in_specs,
        out_specs=pl.BlockSpec((tstep, vw), lambda b, i: (b * nq + i, 0)),
        out_shape=jax.ShapeDtypeStruct((ntok, vw), BF16),
        scratch_shapes=[pltpu.VMEM((3, seq + past, tq), F32)],
        compiler_params=pltpu.CompilerParams(
            dimension_semantics=("arbitrary", "arbitrary"), vmem_limit_bytes=VMEM_LIMIT),
        name="mla_attn",
    )(*args)


def _pair_partner(w):
    k, n = w.shape
    wp = w.reshape(k, n // 2, 2)
    return jnp.stack([-wp[..., 1], wp[..., 0]], axis=-1).reshape(k, n)


def _rope_table(n):
    rows = n // GRID_W
    row = jnp.repeat(jnp.arange(rows, dtype=F32), GRID_W)
    col = jnp.tile(jnp.arange(GRID_W, dtype=F32), rows)
    half = QK_ROPE // 2
    inv = ROPE_THETA ** (-jnp.arange(0, half, 2, dtype=F32) / half)
    ang = jnp.concatenate([row[:, None] * inv, col[:, None] * inv], axis=-1)
    ang = jnp.repeat(ang, 2, axis=-1)
    return jnp.concatenate([jnp.cos(ang), jnp.sin(ang)], axis=-1)


def _identity_rope_table(n):
    return jnp.concatenate([jnp.ones((n, QK_ROPE), F32), jnp.zeros((n, QK_ROPE), F32)], axis=-1)


def kernel(x_prompt, x_sample, state_rglru, cache_ckv, cache_kpe, c, c_ctx, ada_w, ada_b, norm_mix, norm_mlp, mlp_w1, mlp_w2, rg_w_in, rg_conv_w, rg_conv_b, rg_wa, rg_ba, rg_wx, rg_bx, rg_lambda, rg_w_out, mla_w_dqkv, mla_norm_q, mla_norm_kv, mla_w_uq, mla_w_ukv, mla_w_o, final_norm):
    batch, seq, d = x_prompt.shape
    dec_batch, dec_seq, _ = x_sample.shape
    depth = ada_w.shape[0]
    d_rnn = rg_lambda.shape[-1]
    past = cache_ckv.shape[2]
    assert dec_batch == COND_ROWS - SAMPLE_ROW0

    cond = jnp.concatenate(
        [jnp.broadcast_to(c_ctx[None, :], (SAMPLE_ROW0, d)), c], axis=0)
    mod = _ada_call(cond, ada_w, ada_b).reshape(depth, COND_ROWS, 6, d)

    w1 = mlp_w1.astype(BF16)
    w2 = mlp_w2.astype(BF16)
    w_in = jnp.concatenate([rg_w_in[..., :d_rnn], 0.5 * rg_w_in[..., d_rnn:]], axis=-1).astype(BF16)
    w_out = rg_w_out.astype(BF16)
    w_o = mla_w_o.astype(BF16)
    n_rg, n_mla = rg_w_in.shape[0], mla_w_dqkv.shape[0]
    nblk = d_rnn // RG_BS
    wg = (0.5 * jnp.concatenate([rg_wa[:, 0], rg_wa[:, 1], rg_wx[:, 0], rg_wx[:, 1]], axis=-1)).astype(BF16)
    bg = 0.5 * jnp.concatenate(
        [rg_ba[:, 0].reshape(n_rg, nblk, 1, RG_BS), rg_ba[:, 1].reshape(n_rg, nblk, 1, RG_BS),
         rg_bx[:, 0].reshape(n_rg, nblk, 1, RG_BS), rg_bx[:, 1].reshape(n_rg, nblk, 1, RG_BS)], axis=-1)
    bg_hi = bg.astype(BF16)
    bg_lo = (bg - bg_hi.astype(F32)).astype(BF16)
    wg = jnp.concatenate(
        [wg, bg_hi, bg_lo, jnp.zeros((n_rg, nblk, RG_BS - 2, 4 * RG_BS), BF16)], axis=2)
    lo = Q_LORA + KV_LORA
    wd, wuq, wk, wv = [], [], [], []
    for j in range(n_mla):
        w = mla_w_dqkv[j]
        wd.append(jnp.concatenate([w, _pair_partner(w[:, lo:])], axis=1).astype(BF16))
        wq = mla_w_uq[j].reshape(Q_LORA, N_HEADS, QK_NOPE + QK_ROPE)
        wq_pe = wq[..., QK_NOPE:]
        wq_pp = jnp.stack([_pair_partner(wq_pe[:, hd]) for hd in range(N_HEADS)], axis=1)
        wuq.append(jnp.concatenate([wq, wq_pp], axis=-1).reshape(Q_LORA, N_HEADS * HEAD_SLOT).T.astype(BF16))
        wkv = mla_w_ukv[j].reshape(KV_LORA, N_HEADS, QK_NOPE + V_DIM)
        wk.append(wkv[..., :QK_NOPE].reshape(KV_LORA, N_HEADS * QK_NOPE).astype(BF16))
        wv.append(wkv[..., QK_NOPE:].reshape(KV_LORA, N_HEADS * V_DIM).T.astype(BF16))

    fn = final_norm.reshape(1, d)
    tab_sample = _rope_table(dec_seq)
    tm = 512

    def run_stream(x, h0_all, cache, *, sample):
        b, t, _ = x.shape
        tmp = min(tm, t)
        if sample:
            row_of_batch = lambda i: SAMPLE_ROW0 + i
            row_of_tile = lambda i: SAMPLE_ROW0 + i // (t // tm)
            row_of_ptile = lambda i: SAMPLE_ROW0 + i // (t // tmp)
            tab, tab_tile = tab_sample, (lambda i: i % (t // tmp))
        else:
            row_of_batch = lambda i: 0
            row_of_tile = lambda i: 0
            row_of_ptile = lambda i: 0
            tab, tab_tile = _identity_rope_table(tmp), (lambda i: 0)
        tab_t = tab.T
        xt = x.reshape(b * t, d)
        states, ckvs, kpes = [], [], []
        for l in range(depth):
            j = l // 2
            if l % 2 == 0:
                y, fin = _rg_call(xt.reshape(b, t, d), mod, l, row_of_batch, norm_mix[l].reshape(1, d),
                                  w_in, j, rg_conv_w[j], rg_conv_b[j].reshape(1, d_rnn), wg[j],
                                  rg_lambda[j], h0_all[:, j], cblk=256)
                states.append(fin)
                y = y.reshape(b * t, d_rnn)
                wp = w_out
            else:
                outs = _mlaproj_call(xt, mod, l, row_of_ptile, norm_mix[l].reshape(1, d), wd[j],
                                     mla_norm_q[j].reshape(1, Q_LORA), mla_norm_kv[j].reshape(1, KV_LORA),
                                     wuq[j], wk[j], wv[j], tab, tab_t, tab_tile, seq=t, tm=tmp,
                                     emit_cache=not sample)
                q, k, v = outs[:3]
                if sample:
                    ckv_c, kpe_c = cache
                    kpe_p = jnp.pad(kpe_c[:, j].reshape(b * past, QK_ROPE), ((0, 0), (0, LANES - QK_ROPE)))
                    kc, vc = _cachekv_call(ckv_c[:, j].reshape(b * past, KV_LORA), kpe_p, wk[j], wv[j], past=past)
                else:
                    kc = vc = None
                    ckvs.append(outs[3].reshape(b, t, KV_LORA))
                    kpes.append(outs[4].reshape(b, t, QK_ROPE))
                y = _attn_call(q, k, v, kc, vc, tq=min(512, t), nsub=1)
                wp = w_o
            xt = _post_call(xt, y, mod, l, row_of_tile, norm_mlp[l].reshape(1, d), wp, j, w1, w2, fn,
                            tm=tm, final=(l == depth - 1))
        return xt.reshape(b, t, d), states, ckvs, kpes

    zeros_h0 = jnp.zeros((batch, n_rg, 2, d_rnn), F32)
    y_prompt, states, ckvs, kpes = run_stream(x_prompt, zeros_h0, None, sample=False)
    y_sample, _, _, _ = run_stream(x_sample, state_rglru, (cache_ckv, cache_kpe), sample=True)
    return (y_prompt, y_sample, jnp.stack(states, axis=1), jnp.stack(ckvs, axis=1), jnp.stack(kpes, axis=1))
```

```python
import functools

import jax
import jax.numpy as jnp
from jax import lax
from jax.experimental import pallas as pl
from jax.experimental.pallas import tpu as pltpu

F32 = jnp.float32
BF16 = jnp.bfloat16

EPS = 1e-6
LOG2_E = 1.4426950408889634
RG_C = 8.0
RG_BS = 128
N_HEADS = 8
QK_NOPE = 128
QK_ROPE = 64
V_DIM = 128
Q_LORA = 512
KV_LORA = 256
GRID_W = 64
ROPE_THETA = 10000.0
HEAD_SLOT = 256

LANES = 128
SUBLANES = 8
VMEM_LIMIT = 56 * 1024 * 1024

COND_ROWS = 16
SAMPLE_ROW0 = 8


def _dot(a, b):
    return jnp.dot(a, b, preferred_element_type=F32)


def _dot_nt(a, b):
    return lax.dot_general(a, b, (((1,), (1,)), ((), ())), preferred_element_type=F32)


def _rms(x, g):
    ms = jnp.mean(x * x, axis=-1, keepdims=True)
    return x * lax.rsqrt(ms + EPS) * g


def _modnorm(x, g, shift, scale):
    ms = jnp.mean(x * x, axis=-1, keepdims=True)
    return x * lax.rsqrt(ms + EPS) * (g * (1.0 + scale)) + shift


def _sigmoid(x):
    return 0.5 * jnp.tanh(0.5 * x) + 0.5


def _gelu_tanh_of_half(xh):
    c = 0.7978845608028654
    return xh + xh * jnp.tanh(xh * (2.0 * c + (8.0 * c * 0.044715) * (xh * xh)))


def _ada_kernel(cond_ref, w_ref, b_ref, o_ref):
    c = cond_ref[...]
    s = c * _sigmoid(c)
    rows = s.shape[0]
    s_hi = s.astype(BF16)
    s_lo = (s - s_hi.astype(F32)).astype(BF16)
    w = w_ref[0]
    w_hi = w.astype(BF16)
    w_lo = (w - w_hi.astype(F32)).astype(BF16)
    main = _dot(jnp.concatenate([s_hi, s_lo], axis=0), w_hi)
    o_ref[0] = main[:rows] + main[rows:] + _dot(s_hi, w_lo) + b_ref[0]


def _ada_call(cond, ada_w, ada_b):
    depth, d, n6 = ada_w.shape
    tn = 1536
    return pl.pallas_call(
        _ada_kernel,
        grid=(depth, n6 // tn),
        in_specs=[
            pl.BlockSpec((COND_ROWS, d), lambda l, n: (0, 0)),
            pl.BlockSpec((1, d, tn), lambda l, n: (l, 0, n)),
            pl.BlockSpec((1, 1, tn), lambda l, n: (l, 0, n)),
        ],
        out_specs=pl.BlockSpec((1, COND_ROWS, tn), lambda l, n: (l, 0, n)),
        out_shape=jax.ShapeDtypeStruct((depth, COND_ROWS, n6), F32),
        compiler_params=pltpu.CompilerParams(
            dimension_semantics=("arbitrary", "arbitrary"), vmem_limit_bytes=VMEM_LIMIT),
        name="ada",
    )(cond, ada_w, ada_b.reshape(depth, 1, n6))


def _post_kernel(x_ref, y_ref, mod_ref, gn_ref, wp_ref, w1_ref, w2_ref, fn_ref, o_ref, *,
                 ff_chunk, final):
    mod = mod_ref[0, 0]
    g1, sh2, sc2, g2 = mod[2:3], mod[3:4], mod[4:5], mod[5:6]
    x1 = x_ref[...] + g1 * _dot(y_ref[...], wp_ref[...])
    h = _modnorm(x1, gn_ref[...], sh2, sc2).astype(BF16)
    ff = w1_ref.shape[1]
    acc = jnp.zeros_like(x1)
    for c in range(ff // ff_chunk):
        a = jnp.maximum(_dot(h, w1_ref[:, c * ff_chunk:(c + 1) * ff_chunk]), 0.0)
        acc = acc + _dot((a * a).astype(BF16), w2_ref[c * ff_chunk:(c + 1) * ff_chunk, :])
    out = x1 + g2 * acc
    if final:
        out = _rms(out, fn_ref[...])
    o_ref[...] = out


def _post_call(x, y, mod, layer, row_of_tile, gn, wp, wp_index, w1, w2, fn, *, tm, final):
    ntok, d = x.shape
    ff = w1.shape[2]
    const = dict(pipeline_mode=pl.Buffered(1))
    return pl.pallas_call(
        functools.partial(_post_kernel, ff_chunk=1024, final=final),
        grid=(ntok // tm,),
        in_specs=[
            pl.BlockSpec((tm, d), lambda i: (i, 0)),
            pl.BlockSpec((tm, d), lambda i: (i, 0)),
            pl.BlockSpec((1, 1, 6, d), lambda i: (layer, row_of_tile(i), 0, 0)),
            pl.BlockSpec((1, d), lambda i: (0, 0)),
            pl.BlockSpec((None, d, d), lambda i: (wp_index, 0, 0), **const),
            pl.BlockSpec((None, d, ff), lambda i: (layer, 0, 0), **const),
            pl.BlockSpec((None, ff, d), lambda i: (layer, 0, 0), **const),
            pl.BlockSpec((1, d), lambda i: (0, 0)),
        ],
        out_specs=pl.BlockSpec((tm, d), lambda i: (i, 0)),
        out_shape=jax.ShapeDtypeStruct((ntok, d), F32),
        compiler_params=pltpu.CompilerParams(
            dimension_semantics=("arbitrary",), vmem_limit_bytes=VMEM_LIMIT),
        name="post",
    )(x, y, mod, gn, wp, w1, w2, fn)


def _rg_kernel(x_ref, mod_ref, gn_ref, wx_ref, wy_ref, cw_ref, cb_ref, wg_ref, lam_ref,
               h0_ref, y_ref, fin_ref,
               h_scr, xb_scr, yb_scr, af_scr, bf_scr, ab_scr, bb_scr, hf_scr, hb_scr, *,
               seq, seg, chunk):
    nb = x_ref.shape[0]
    n_lane_grp = wx_ref.shape[1] // LANES
    n_slab = nb * n_lane_grp
    n_chunk = seq // chunk
    pad = SUBLANES * seg - seq
    halo = SUBLANES

    @pl.when(pl.program_id(1) == 0)
    def _():
        mod = mod_ref[0, 0]
        for bi in range(nb):
            for c in range(n_chunk):
                rows = slice(c * chunk, (c + 1) * chunk)
                h_scr[bi, rows, :] = _modnorm(
                    x_ref[bi, rows, :], gn_ref[...], mod[0:1], mod[1:2]).astype(BF16)

    def xb_rows(t0, n):
        return pl.ds(2 * (halo + t0), n, stride=2)

    for sl in range(n_slab):
        xb_scr[sl, 0:2 * halo, :] = jnp.zeros((2 * halo, LANES), F32)
        xb_scr[sl, 2 * (halo + seq):2 * (2 * halo + seq), :] = jnp.zeros((2 * halo, LANES), F32)
        tail = slice(seq, seq + pad)
        af_scr[sl, tail, :] = jnp.ones((pad, LANES), F32)
        ab_scr[sl, tail, :] = jnp.ones((pad, LANES), F32)
        bf_scr[sl, tail, :] = jnp.zeros((pad, LANES), F32)
        bb_scr[sl, tail, :] = jnp.zeros((pad, LANES), F32)

    def project(bi, c):
        rows = slice(c * chunk, (c + 1) * chunk)
        hc = h_scr[bi, rows, :]
        ux = _dot(hc, wx_ref[...])
        for s in range(n_lane_grp):
            xb_scr[bi * n_lane_grp + s, xb_rows(c * chunk, chunk), :] = ux[:, s * LANES:(s + 1) * LANES]
        yb_scr[bi, rows, :] = _gelu_tanh_of_half(_dot(hc, wy_ref[...]))

    cw = cw_ref[...]
    lam = lam_ref[...]
    c1_all = (-0.5 * RG_C * LOG2_E) * (jnp.maximum(-lam, 0.0) + jnp.log1p(jnp.exp(-jnp.abs(lam))))
    lane = lax.broadcasted_iota(jnp.int32, (chunk, LANES), 1)
    bias_cols = jnp.where(lane < 2, 1.0, 0.0).astype(BF16)

    def gates(bi, c):
        rows = slice(c * chunk, (c + 1) * chunk)
        for s in range(n_lane_grp):
            sl = bi * n_lane_grp + s
            lanes = slice(s * LANES, (s + 1) * LANES)
            xs = cb_ref[:, lanes] + cw[0:1, lanes] * xb_scr[sl, xb_rows(c * chunk - 1, chunk), :]
            for k in range(1, 4):
                xs = xs + cw[k:k + 1, lanes] * xb_scr[sl, xb_rows(c * chunk - 1 + k, chunk), :]
            g = _dot(jnp.concatenate([xs.astype(BF16), bias_cols], axis=1), wg_ref[s])
            xh = 0.5 * xs
            for d, (a_scr, b_scr) in enumerate(((af_scr, bf_scr), (ab_scr, bb_scr))):
                c1 = c1_all[d:d + 1, lanes]
                tr = jnp.tanh(g[:, d * LANES:(d + 1) * LANES])
                ti = jnp.tanh(g[:, (2 + d) * LANES:(3 + d) * LANES])
                a = jnp.exp2(c1 * tr + c1)
                a_scr[sl, rows, :] = a
                v = 1.0 - a * a
                root = jnp.where(v > 0.0, v * lax.rsqrt(v), 0.0)
                b_scr[sl, rows, :] = root * (ti * xh + xh)

    for bi in range(nb):
        project(bi, 0)
        for c in range(1, n_chunk):
            project(bi, c)
            gates(bi, c - 1)
        gates(bi, n_chunk - 1)

    def seg_rows(j):
        return pl.ds(j, SUBLANES, stride=seg)

    def run_steps(step, init):
        unroll = 8
        main = seg // unroll

        def body(i, carry):
            for u in range(unroll):
                carry = step(i * unroll + u, carry)
            return carry

        carry = lax.fori_loop(0, main, body, init)
        for j in range(main * unroll, seg):
            carry = step(j, carry)
        return carry

    def pass1(j, carry):
        jb = seg - 1 - j
        out = []
        for sl in range(n_slab):
            pf, hf, pb, hb = carry[sl]
            a = af_scr[sl, seg_rows(j), :]
            b = bf_scr[sl, seg_rows(j), :]
            a2 = ab_scr[sl, seg_rows(jb), :]
            b2 = bb_scr[sl, seg_rows(jb), :]
            out.append((pf * a, a * hf + b, pb * a2, a2 * hb + b2))
        return tuple(out)

    one = jnp.ones((SUBLANES, LANES), F32)
    zero = jnp.zeros((SUBLANES, LANES), F32)
    totals = run_steps(pass1, tuple((one, zero, one, zero) for _ in range(n_slab)))

    rid = lax.broadcasted_iota(jnp.int32, (SUBLANES, LANES), 0)
    init_f, init_b = [], []
    for sl in range(n_slab):
        bi, s = divmod(sl, n_lane_grp)
        lanes = slice(s * LANES, (s + 1) * LANES)
        pf, hf, pb, hb = totals[sl]
        st = h0_ref[bi, 0:1, lanes]
        ini = zero
        for k in range(SUBLANES):
            ini = jnp.where(rid == k, st, ini)
            st = hf[k:k + 1] + pf[k:k + 1] * st
        init_f.append(ini)
        fin_ref[bi, 0:1, lanes] = st
        st = h0_ref[bi, 1:2, lanes]
        ini = zero
        for k in range(SUBLANES - 1, -1, -1):
            ini = jnp.where(rid == k, st, ini)
            st = hb[k:k + 1] + pb[k:k + 1] * st
        init_b.append(ini)
        fin_ref[bi, 1:2, lanes] = st

    def pass2(j, carry):
        jb = seg - 1 - j
        out = []
        for sl in range(n_slab):
            hf, hb = carry[sl]
            hf = af_scr[sl, seg_rows(j), :] * hf + bf_scr[sl, seg_rows(j), :]
            hf_scr[sl, seg_rows(j), :] = hf
            hb = ab_scr[sl, seg_rows(jb), :] * hb + bb_scr[sl, seg_rows(jb), :]
            hb_scr[sl, seg_rows(jb), :] = hb
            out.append((hf, hb))
        return tuple(out)

    run_steps(pass2, tuple(zip(init_f, init_b)))

    for sl in range(n_slab):
        bi, s = divmod(sl, n_lane_grp)
        lanes = slice(s * LANES, (s + 1) * LANES)
        for c in range(n_chunk):
            rows = slice(c * chunk, (c + 1) * chunk)
            y_ref[bi, rows, lanes] = (
                (hf_scr[sl, rows, :] + hb_scr[sl, rows, :]) * yb_scr[bi, rows, lanes]).astype(BF16)


def _odd_segment_length(seq):
    seg = -(-seq // SUBLANES)
    return seg if seg % 2 == 1 else seg + 1


def _rg_call(x, mod, layer, row_of_group, gn, w_in, w_index, conv_w, conv_b, wg, lam, h0, *, cblk, nb):
    b, seq, d = x.shape
    d_rnn = lam.shape[1]
    ncb = d_rnn // cblk
    n_lane_grp = cblk // LANES
    n_slab = nb * n_lane_grp
    seg = _odd_segment_length(seq)
    seq_pad = SUBLANES * seg
    chunk = min(seq, 512)
    slab = pltpu.VMEM((n_slab, seq_pad, LANES), F32)
    return pl.pallas_call(
        functools.partial(_rg_kernel, seq=seq, seg=seg, chunk=chunk),
        grid=(b // nb, ncb),
        in_specs=[
            pl.BlockSpec((nb, seq, d), lambda i, c: (i, 0, 0)),
            pl.BlockSpec((1, 1, 6, d), lambda i, c: (layer, row_of_group(i), 0, 0)),
            pl.BlockSpec((1, d), lambda i, c: (0, 0)),
            pl.BlockSpec((None, d, cblk), lambda i, c: (w_index, 0, c)),
            pl.BlockSpec((None, d, cblk), lambda i, c: (w_index, 0, ncb + c)),
            pl.BlockSpec((4, cblk), lambda i, c: (0, c)),
            pl.BlockSpec((1, cblk), lambda i, c: (0, c)),
            pl.BlockSpec((n_lane_grp, 2 * RG_BS, 4 * RG_BS), lambda i, c: (c, 0, 0)),
            pl.BlockSpec((2, cblk), lambda i, c: (0, c)),
            pl.BlockSpec((nb, 2, cblk), lambda i, c: (i, 0, c)),
        ],
        out_specs=[
            pl.BlockSpec((nb, seq, cblk), lambda i, c: (i, 0, c)),
            pl.BlockSpec((nb, 2, cblk), lambda i, c: (i, 0, c)),
        ],
        out_shape=[
            jax.ShapeDtypeStruct((b, seq, d_rnn), BF16),
            jax.ShapeDtypeStruct((b, 2, d_rnn), F32),
        ],
        scratch_shapes=[
            pltpu.VMEM((nb, seq, d), BF16),
            pltpu.VMEM((n_slab, 2 * (seq + 2 * SUBLANES), LANES), F32),
            pltpu.VMEM((nb, seq, cblk), F32),
            slab, slab, slab, slab,
            slab, slab,
        ],
        compiler_params=pltpu.CompilerParams(
            dimension_semantics=("arbitrary", "arbitrary"), vmem_limit_bytes=VMEM_LIMIT),
        name="rg_core",
    )(x, mod, gn, w_in, w_in, conv_w, conv_b, wg, lam, h0)


def _rope_pairs(v, tab):
    p = v * tab
    return p + pltpu.roll(p, QK_ROPE, axis=1)


def _mlaproj_kernel(x_ref, mod_ref, gn_ref, wd_ref, gq_ref, gkv_ref, wuq_ref, wk_ref, wv_ref, tab_ref,
                    tabt_ref, q_ref, k_ref, v_ref, *cache_refs, q_scale):
    mod = mod_ref[0, 0]
    h = _modnorm(x_ref[...], gn_ref[...], mod[0:1], mod[1:2]).astype(BF16)
    proj = _dot(h, wd_ref[...])
    cq = _rms(proj[:, :Q_LORA], gq_ref[...])
    ckv = _rms(proj[:, Q_LORA:Q_LORA + KV_LORA], gkv_ref[...])
    kp2 = proj[:, Q_LORA + KV_LORA:]
    if cache_refs:
        ckv_ref, kpe_ref = cache_refs
        ckv_ref[...] = ckv
        kpe_ref[...] = kp2[:, :QK_ROPE]
    tab = tab_ref[...]
    lane = lax.broadcasted_iota(jnp.int32, kp2.shape, 1)
    kr = jnp.where(lane < QK_ROPE, _rope_pairs(kp2, tab), 0.0).astype(BF16)
    qt = _dot_nt(wuq_ref[...], cq.astype(BF16))
    ckvb = ckv.astype(BF16)
    kn = _dot(ckvb, wk_ref[...])
    v_ref[0] = _dot_nt(wv_ref[...], ckvb).astype(BF16)
    tabt = tabt_ref[...] * q_scale
    for hd in range(N_HEADS):
        lo = hd * HEAD_SLOT
        q_ref[0, lo:lo + QK_NOPE, :] = (qt[lo:lo + QK_NOPE] * q_scale).astype(BF16)
        pr = qt[lo + QK_NOPE:lo + HEAD_SLOT] * tabt
        qr = (pr[:QK_ROPE] + pr[QK_ROPE:]).astype(BF16)
        q_ref[0, lo + QK_NOPE:lo + QK_NOPE + QK_ROPE, :] = qr
        q_ref[0, lo + QK_NOPE + QK_ROPE:lo + HEAD_SLOT, :] = qr
        k_ref[:, lo:lo + QK_NOPE] = kn[:, hd * QK_NOPE:(hd + 1) * QK_NOPE].astype(BF16)
        k_ref[:, lo + QK_NOPE:lo + HEAD_SLOT] = kr


def _mlaproj_call(x, mod, layer, row_of_tile, gn, wd, gq, gkv, wuq_t, wk, wv_t, tab, tab_t, tab_tile, *,
                  seq, tm, emit_cache):
    ntok, d = x.shape
    nt = seq // tm
    const = dict(pipeline_mode=pl.Buffered(1))
    out_specs = [
        pl.BlockSpec((1, N_HEADS * HEAD_SLOT, tm), lambda i: (i // nt, 0, i % nt)),
        pl.BlockSpec((tm, N_HEADS * HEAD_SLOT), lambda i: (i, 0)),
        pl.BlockSpec((1, N_HEADS * V_DIM, tm), lambda i: (i // nt, 0, i % nt)),
    ]
    out_shape = [
        jax.ShapeDtypeStruct((ntok // seq, N_HEADS * HEAD_SLOT, seq), BF16),
        jax.ShapeDtypeStruct((ntok, N_HEADS * HEAD_SLOT), BF16),
        jax.ShapeDtypeStruct((ntok // seq, N_HEADS * V_DIM, seq), BF16),
    ]
    if emit_cache:
        out_specs += [pl.BlockSpec((tm, KV_LORA), lambda i: (i, 0)),
                      pl.BlockSpec((tm, QK_ROPE), lambda i: (i, 0))]
        out_shape += [jax.ShapeDtypeStruct((ntok, KV_LORA), F32),
                      jax.ShapeDtypeStruct((ntok, QK_ROPE), F32)]
    return pl.pallas_call(
        functools.partial(_mlaproj_kernel, q_scale=float((QK_NOPE + QK_ROPE) ** -0.5 * LOG2_E)),
        grid=(ntok // tm,),
        in_specs=[
            pl.BlockSpec((tm, d), lambda i: (i, 0)),
            pl.BlockSpec((1, 1, 6, d), lambda i: (layer, row_of_tile(i), 0, 0)),
            pl.BlockSpec((1, d), lambda i: (0, 0)),
            pl.BlockSpec(wd.shape, lambda i: (0, 0), **const),
            pl.BlockSpec((1, Q_LORA), lambda i: (0, 0)),
            pl.BlockSpec((1, KV_LORA), lambda i: (0, 0)),
            pl.BlockSpec(wuq_t.shape, lambda i: (0, 0), **const),
            pl.BlockSpec(wk.shape, lambda i: (0, 0), **const),
            pl.BlockSpec(wv_t.shape, lambda i: (0, 0), **const),
            pl.BlockSpec((tm, LANES), lambda i: (tab_tile(i), 0)),
            pl.BlockSpec((LANES, tm), lambda i: (0, tab_tile(i))),
        ],
        out_specs=out_specs,
        out_shape=out_shape,
        compiler_params=pltpu.CompilerParams(
            dimension_semantics=("arbitrary",), vmem_limit_bytes=VMEM_LIMIT),
        name="mla_proj",
    )(x, mod, gn, wd, gq, gkv, wuq_t, wk, wv_t, tab, tab_t)


def _cachekv_kernel(ckv_ref, kpe_ref, wk_ref, wv_ref, k_ref, v_ref):
    ckvb = ckv_ref[...].astype(BF16)
    kn = _dot(ckvb, wk_ref[...])
    v_ref[0] = _dot_nt(wv_ref[...], ckvb).astype(BF16)
    kp = kpe_ref[...].astype(BF16)
    for hd in range(N_HEADS):
        lo = hd * HEAD_SLOT
        k_ref[:, lo:lo + QK_NOPE] = kn[:, hd * QK_NOPE:(hd + 1) * QK_NOPE].astype(BF16)
        k_ref[:, lo + QK_NOPE:lo + HEAD_SLOT] = kp


def _cachekv_call(ckv, kpe_padded, wk, wv_t, *, past):
    ntok = ckv.shape[0]
    return pl.pallas_call(
        _cachekv_kernel,
        grid=(ntok // past,),
        in_specs=[
            pl.BlockSpec((past, KV_LORA), lambda i: (i, 0)),
            pl.BlockSpec((past, LANES), lambda i: (i, 0)),
            pl.BlockSpec(wk.shape, lambda i: (0, 0)),
            pl.BlockSpec(wv_t.shape, lambda i: (0, 0)),
        ],
        out_specs=[
            pl.BlockSpec((past, N_HEADS * HEAD_SLOT), lambda i: (i, 0)),
            pl.BlockSpec((1, N_HEADS * V_DIM, past), lambda i: (i, 0, 0)),
        ],
        out_shape=[
            jax.ShapeDtypeStruct((ntok, N_HEADS * HEAD_SLOT), BF16),
            jax.ShapeDtypeStruct((ntok // past, N_HEADS * V_DIM, past), BF16),
        ],
        compiler_params=pltpu.CompilerParams(
            dimension_semantics=("arbitrary",), vmem_limit_bytes=VMEM_LIMIT),
        name="mla_cache_kv",
    )(ckv, kpe_padded, wk, wv_t)


def _attn_kernel(q_ref, k_ref, v_ref, *rest):
    if len(rest) == 4:
        kc_ref, vc_ref, o_ref, s_scr = rest
    else:
        kc_ref = vc_ref = None
        o_ref, s_scr = rest
    kchunk = min(512, k_ref.shape[1])
    chunks = [(k_ref, v_ref, r, r, kchunk) for r in range(0, k_ref.shape[1], kchunk)]
    if kc_ref is not None:
        cchunk = min(kchunk, kc_ref.shape[1])
        chunks += [(kc_ref, vc_ref, r, k_ref.shape[1] + r, cchunk) for r in range(0, kc_ref.shape[1], cchunk)]

    tq = s_scr.shape[2]
    n_sub = q_ref.shape[2] // tq
    n_item = q_ref.shape[0] * n_sub * N_HEADS

    def item_index(it):
        return it // (n_sub * N_HEADS), (it // N_HEADS) % n_sub, it % N_HEADS

    def score_chunk(it, chunk):
        kref, _, r, srow, n = chunk
        bi, sub, hd = item_index(it)
        slot = slice(hd * HEAD_SLOT, (hd + 1) * HEAD_SLOT)
        s = _dot(kref[bi, r:r + n, slot], q_ref[bi, slot, sub * tq:(sub + 1) * tq])
        s_scr[it % n_buf, srow:srow + n, :] = s
        return jnp.max(s, axis=0, keepdims=True)

    def value_chunk(it, chunk, m):
        _, vref, r, srow, n = chunk
        bi, _, hd = item_index(it)
        p = jnp.exp2(s_scr[it % n_buf, srow:srow + n, :] - m)
        ov = _dot(vref[bi, hd * V_DIM:(hd + 1) * V_DIM, r:r + n], p.astype(BF16))
        return jnp.sum(p, axis=0, keepdims=True), ov

    def running_max(acc, new):
        return new if acc is None else jnp.maximum(acc, new)

    n_buf = s_scr.shape[0]
    nc = len(chunks)
    ahead = nc + min(nc, 3)
    assert n_buf == 3 and nc < ahead <= 2 * nc
    total = n_item * nc
    maxes = {}

    def issue_scores(i):
        it, c = divmod(i, nc)
        maxes[it] = running_max(maxes.get(it), score_chunk(it, chunks[c]))

    for i in range(ahead):
        issue_scores(i)
    l, ov = None, None
    for j in range(total):
        if j + ahead < total:
            issue_scores(j + ahead)
        it, c = divmod(j, nc)
        l_c, ov_c = value_chunk(it, chunks[c], maxes[it])
        l = l_c if c == 0 else l + l_c
        ov = ov_c if c == 0 else ov + ov_c
        if c == nc - 1:
            bi, sub, hd = item_index(it)
            row0 = (bi * n_sub + sub) * tq
            o_ref[row0:row0 + tq, hd * V_DIM:(hd + 1) * V_DIM] = (ov * (1.0 / l)).T.astype(BF16)


def _attn_call(q_t, k, v_t, kc, vc_t, *, tq, nsub, nb):
    batch, kw, seq = q_t.shape
    ntok = batch * seq
    tstep = tq * nsub
    nq = seq // tstep
    assert nb == 1 or nq == 1
    vw = N_HEADS * V_DIM
    args = [q_t, k.reshape(batch, seq, kw), v_t]
    in_specs = [
        pl.BlockSpec((nb, kw, tstep), lambda b, i: (b, 0, i)),
        pl.BlockSpec((nb, seq, kw), lambda b, i: (b, 0, 0)),
        pl.BlockSpec((nb, vw, seq), lambda b, i: (b, 0, 0)),
    ]
    past = 0
    if kc is not None:
        past = kc.shape[0] // batch
        args += [kc.reshape(batch, past, kw), vc_t]
        in_specs += [pl.BlockSpec((nb, past, kw), lambda b, i: (b, 0, 0)),
                     pl.BlockSpec((nb, vw, past), lambda b, i: (b, 0, 0))]
    return pl.pallas_call(
        _attn_kernel,
        grid=(batch // nb, nq),
        in_specs=in_specs,
        out_specs=pl.BlockSpec((nb * tstep, vw), lambda b, i: (b * nq + i, 0)),
        out_shape=jax.ShapeDtypeStruct((ntok, vw), BF16),
        scratch_shapes=[pltpu.VMEM((3, seq + past, tq), F32)],
        compiler_params=pltpu.CompilerParams(
            dimension_semantics=("arbitrary", "arbitrary"), vmem_limit_bytes=VMEM_LIMIT),
        name="mla_attn",
    )(*args)


def _pair_partner(w):
    k, n = w.shape
    wp = w.reshape(k, n // 2, 2)
    return jnp.stack([-wp[..., 1], wp[..., 0]], axis=-1).reshape(k, n)


def _rope_table(n):
    rows = n // GRID_W
    row = jnp.repeat(jnp.arange(rows, dtype=F32), GRID_W)
    col = jnp.tile(jnp.arange(GRID_W, dtype=F32), rows)
    half = QK_ROPE // 2
    inv = ROPE_THETA ** (-jnp.arange(0, half, 2, dtype=F32) / half)
    ang = jnp.concatenate([row[:, None] * inv, col[:, None] * inv], axis=-1)
    ang = jnp.repeat(ang, 2, axis=-1)
    return jnp.concatenate([jnp.cos(ang), jnp.sin(ang)], axis=-1)


def _identity_rope_table(n):
    return jnp.concatenate([jnp.ones((n, QK_ROPE), F32), jnp.zeros((n, QK_ROPE), F32)], axis=-1)


def kernel(x_prompt, x_sample, state_rglru, cache_ckv, cache_kpe, c, c_ctx, ada_w, ada_b, norm_mix, norm_mlp, mlp_w1, mlp_w2, rg_w_in, rg_conv_w, rg_conv_b, rg_wa, rg_ba, rg_wx, rg_bx, rg_lambda, rg_w_out, mla_w_dqkv, mla_norm_q, mla_norm_kv, mla_w_uq, mla_w_ukv, mla_w_o, final_norm):
    batch, seq, d = x_prompt.shape
    dec_batch, dec_seq, _ = x_sample.shape
    depth = ada_w.shape[0]
    d_rnn = rg_lambda.shape[-1]
    past = cache_ckv.shape[2]
    assert dec_batch == COND_ROWS - SAMPLE_ROW0

    cond = jnp.concatenate(
        [jnp.broadcast_to(c_ctx[None, :], (SAMPLE_ROW0, d)), c], axis=0)
    mod = _ada_call(cond, ada_w, ada_b).reshape(depth, COND_ROWS, 6, d)

    w1 = mlp_w1.astype(BF16)
    w2 = mlp_w2.astype(BF16)
    w_in = jnp.concatenate([rg_w_in[..., :d_rnn], 0.5 * rg_w_in[..., d_rnn:]], axis=-1).astype(BF16)
    w_out = rg_w_out.astype(BF16)
    w_o = mla_w_o.astype(BF16)
    n_rg, n_mla = rg_w_in.shape[0], mla_w_dqkv.shape[0]
    nblk = d_rnn // RG_BS
    wg = (0.5 * jnp.concatenate([rg_wa[:, 0], rg_wa[:, 1], rg_wx[:, 0], rg_wx[:, 1]], axis=-1)).astype(BF16)
    bg = 0.5 * jnp.concatenate(
        [rg_ba[:, 0].reshape(n_rg, nblk, 1, RG_BS), rg_ba[:, 1].reshape(n_rg, nblk, 1, RG_BS),
         rg_bx[:, 0].reshape(n_rg, nblk, 1, RG_BS), rg_bx[:, 1].reshape(n_rg, nblk, 1, RG_BS)], axis=-1)
    bg_hi = bg.astype(BF16)
    bg_lo = (bg - bg_hi.astype(F32)).astype(BF16)
    wg = jnp.concatenate(
        [wg, bg_hi, bg_lo, jnp.zeros((n_rg, nblk, RG_BS - 2, 4 * RG_BS), BF16)], axis=2)
    lo = Q_LORA + KV_LORA
    wd, wuq, wk, wv = [], [], [], []
    for j in range(n_mla):
        w = mla_w_dqkv[j]
        wd.append(jnp.concatenate([w, _pair_partner(w[:, lo:])], axis=1).astype(BF16))
        wq = mla_w_uq[j].reshape(Q_LORA, N_HEADS, QK_NOPE + QK_ROPE)
        wq_pe = wq[..., QK_NOPE:]
        wq_pp = jnp.stack([_pair_partner(wq_pe[:, hd]) for hd in range(N_HEADS)], axis=1)
        wuq.append(jnp.concatenate([wq, wq_pp], axis=-1).reshape(Q_LORA, N_HEADS * HEAD_SLOT).T.astype(BF16))
        wkv = mla_w_ukv[j].reshape(KV_LORA, N_HEADS, QK_NOPE + V_DIM)
        wk.append(wkv[..., :QK_NOPE].reshape(KV_LORA, N_HEADS * QK_NOPE).astype(BF16))
        wv.append(wkv[..., QK_NOPE:].reshape(KV_LORA, N_HEADS * V_DIM).T.astype(BF16))

    fn = final_norm.reshape(1, d)
    tab_sample = _rope_table(dec_seq)
    tm = 512

    def run_stream(x, h0_all, cache, *, sample):
        b, t, _ = x.shape
        tmp = min(tm, t)
        rg_group = 1 if sample else 4
        if sample:
            row_of_batch = lambda i: SAMPLE_ROW0 + i
            row_of_tile = lambda i: SAMPLE_ROW0 + i // (t // tm)
            row_of_ptile = lambda i: SAMPLE_ROW0 + i // (t // tmp)
            tab, tab_tile = tab_sample, (lambda i: i % (t // tmp))
        else:
            row_of_batch = lambda i: 0
            row_of_tile = lambda i: 0
            row_of_ptile = lambda i: 0
            tab, tab_tile = _identity_rope_table(tmp), (lambda i: 0)
        tab_t = tab.T
        xt = x.reshape(b * t, d)
        states, ckvs, kpes = [], [], []
        for l in range(depth):
            j = l // 2
            if l % 2 == 0:
                y, fin = _rg_call(xt.reshape(b, t, d), mod, l, row_of_batch, norm_mix[l].reshape(1, d),
                                  w_in, j, rg_conv_w[j], rg_conv_b[j].reshape(1, d_rnn), wg[j],
                                  rg_lambda[j], h0_all[:, j], cblk=256, nb=rg_group)
                states.append(fin)
                y = y.reshape(b * t, d_rnn)
                wp = w_out
            else:
                outs = _mlaproj_call(xt, mod, l, row_of_ptile, norm_mix[l].reshape(1, d), wd[j],
                                     mla_norm_q[j].reshape(1, Q_LORA), mla_norm_kv[j].reshape(1, KV_LORA),
                                     wuq[j], wk[j], wv[j], tab, tab_t, tab_tile, seq=t, tm=tmp,
                                     emit_cache=not sample)
                q, k, v = outs[:3]
                if sample:
                    ckv_c, kpe_c = cache
                    kpe_p = jnp.pad(kpe_c[:, j].reshape(b * past, QK_ROPE), ((0, 0), (0, LANES - QK_ROPE)))
                    kc, vc = _cachekv_call(ckv_c[:, j].reshape(b * past, KV_LORA), kpe_p, wk[j], wv[j], past=past)
                else:
                    kc = vc = None
                    ckvs.append(outs[3].reshape(b, t, KV_LORA))
                    kpes.append(outs[4].reshape(b, t, QK_ROPE))
                y = _attn_call(q, k, v, kc, vc, tq=min(512, t), nsub=1, nb=1 if sample else 4)
                wp = w_o
            xt = _post_call(xt, y, mod, l, row_of_tile, norm_mlp[l].reshape(1, d), wp, j, w1, w2, fn,
                            tm=tm, final=(l == depth - 1))
        return xt.reshape(b, t, d), states, ckvs, kpes

    zeros_h0 = jnp.zeros((batch, n_rg, 2, d_rnn), F32)
    y_prompt, states, ckvs, kpes = run_stream(x_prompt, zeros_h0, None, sample=False)
    y_sample, _, _, _ = run_stream(x_sample, state_rglru, (cache_ckv, cache_kpe), sample=True)
    return (y_prompt, y_sample, jnp.stack(states, axis=1), jnp.stack(ckvs, axis=1), jnp.stack(kpes, axis=1))
```

```python
import functools

import jax
import jax.numpy as jnp
from jax import lax
from jax.experimental import pallas as pl
from jax.experimental.pallas import tpu as pltpu

F32 = jnp.float32
BF16 = jnp.bfloat16

EPS = 1e-6
LOG2_E = 1.4426950408889634
RG_C = 8.0
RG_BS = 128
N_HEADS = 8
QK_NOPE = 128
QK_ROPE = 64
V_DIM = 128
Q_LORA = 512
KV_LORA = 256
GRID_W = 64
ROPE_THETA = 10000.0
HEAD_SLOT = 256

LANES = 128
SUBLANES = 8
VMEM_LIMIT = 56 * 1024 * 1024

COND_ROWS = 16
SAMPLE_ROW0 = 8


def _dot(a, b):
    return jnp.dot(a, b, preferred_element_type=F32)


def _dot_nt(a, b):
    return lax.dot_general(a, b, (((1,), (1,)), ((), ())), preferred_element_type=F32)


def _rms(x, g):
    ms = jnp.mean(x * x, axis=-1, keepdims=True)
    return x * lax.rsqrt(ms + EPS) * g


def _modnorm(x, g, shift, scale):
    ms = jnp.mean(x * x, axis=-1, keepdims=True)
    return x * lax.rsqrt(ms + EPS) * (g * (1.0 + scale)) + shift


def _sigmoid(x):
    return 0.5 * jnp.tanh(0.5 * x) + 0.5


def _gelu_tanh_of_half(xh):
    c = 0.7978845608028654
    return xh + xh * jnp.tanh(xh * (2.0 * c + (8.0 * c * 0.044715) * (xh * xh)))


def _ada_kernel(cond_ref, w_ref, b_ref, o_ref):
    c = cond_ref[...]
    s = c * _sigmoid(c)
    rows = s.shape[0]
    s_hi = s.astype(BF16)
    s_lo = (s - s_hi.astype(F32)).astype(BF16)
    w = w_ref[0]
    w_hi = w.astype(BF16)
    w_lo = (w - w_hi.astype(F32)).astype(BF16)
    main = _dot(jnp.concatenate([s_hi, s_lo], axis=0), w_hi)
    o_ref[0] = main[:rows] + main[rows:] + _dot(s_hi, w_lo) + b_ref[0]


def _ada_call(cond, ada_w, ada_b):
    depth, d, n6 = ada_w.shape
    tn = 1536
    return pl.pallas_call(
        _ada_kernel,
        grid=(depth, n6 // tn),
        in_specs=[
            pl.BlockSpec((COND_ROWS, d), lambda l, n: (0, 0)),
            pl.BlockSpec((1, d, tn), lambda l, n: (l, 0, n)),
            pl.BlockSpec((1, 1, tn), lambda l, n: (l, 0, n)),
        ],
        out_specs=pl.BlockSpec((1, COND_ROWS, tn), lambda l, n: (l, 0, n)),
        out_shape=jax.ShapeDtypeStruct((depth, COND_ROWS, n6), F32),
        compiler_params=pltpu.CompilerParams(
            dimension_semantics=("arbitrary", "arbitrary"), vmem_limit_bytes=VMEM_LIMIT),
        name="ada",
    )(cond, ada_w, ada_b.reshape(depth, 1, n6))


def _post_kernel(x_ref, y_ref, mod_ref, gn_ref, wp_ref, w1_ref, w2_ref, fn_ref, o_ref, *,
                 ff_chunk, final):
    mod = mod_ref[0, 0]
    g1, sh2, sc2, g2 = mod[2:3], mod[3:4], mod[4:5], mod[5:6]
    x1 = x_ref[...] + g1 * _dot(y_ref[...], wp_ref[...])
    h = _modnorm(x1, gn_ref[...], sh2, sc2).astype(BF16)
    ff = w1_ref.shape[1]
    acc = jnp.zeros_like(x1)
    for c in range(ff // ff_chunk):
        a = jnp.maximum(_dot(h, w1_ref[:, c * ff_chunk:(c + 1) * ff_chunk]), 0.0)
        acc = acc + _dot((a * a).astype(BF16), w2_ref[c * ff_chunk:(c + 1) * ff_chunk, :])
    out = x1 + g2 * acc
    if final:
        out = _rms(out, fn_ref[...])
    o_ref[...] = out


def _post_call(x, y, mod, layer, row_of_tile, gn, wp, wp_index, w1, w2, fn, *, tm, final):
    ntok, d = x.shape
    ff = w1.shape[2]
    const = dict(pipeline_mode=pl.Buffered(1))
    return pl.pallas_call(
        functools.partial(_post_kernel, ff_chunk=1024, final=final),
        grid=(ntok // tm,),
        in_specs=[
            pl.BlockSpec((tm, d), lambda i: (i, 0)),
            pl.BlockSpec((tm, d), lambda i: (i, 0)),
            pl.BlockSpec((1, 1, 6, d), lambda i: (layer, row_of_tile(i), 0, 0)),
            pl.BlockSpec((1, d), lambda i: (0, 0)),
            pl.BlockSpec((None, d, d), lambda i: (wp_index, 0, 0), **const),
            pl.BlockSpec((None, d, ff), lambda i: (layer, 0, 0), **const),
            pl.BlockSpec((None, ff, d), lambda i: (layer, 0, 0), **const),
            pl.BlockSpec((1, d), lambda i: (0, 0)),
        ],
        out_specs=pl.BlockSpec((tm, d), lambda i: (i, 0)),
        out_shape=jax.ShapeDtypeStruct((ntok, d), F32),
        compiler_params=pltpu.CompilerParams(
            dimension_semantics=("arbitrary",), vmem_limit_bytes=VMEM_LIMIT),
        name="post",
    )(x, y, mod, gn, wp, w1, w2, fn)


def _rg_kernel(x_ref, mod_ref, gn_ref, wx_ref, wy_ref, cw_ref, cb_ref, wg_ref, lam_ref,
               h0_ref, y_ref, fin_ref,
               h_scr, xb_scr, yb_scr, af_scr, bf_scr, ab_scr, bb_scr, hf_scr, hb_scr, *,
               seq, seg, chunk):
    nb = x_ref.shape[0]
    n_lane_grp = wx_ref.shape[1] // LANES
    n_slab = nb * n_lane_grp
    n_chunk = seq // chunk
    pad = SUBLANES * seg - seq
    halo = SUBLANES

    @pl.when(pl.program_id(1) == 0)
    def _():
        mod = mod_ref[0, 0]
        for bi in range(nb):
            for c in range(n_chunk):
                rows = slice(c * chunk, (c + 1) * chunk)
                h_scr[bi, rows, :] = _modnorm(
                    x_ref[bi, rows, :], gn_ref[...], mod[0:1], mod[1:2]).astype(BF16)

    def xb_rows(t0, n):
        return pl.ds(2 * (halo + t0), n, stride=2)

    for sl in range(n_slab):
        xb_scr[sl, 0:2 * halo, :] = jnp.zeros((2 * halo, LANES), F32)
        xb_scr[sl, 2 * (halo + seq):2 * (2 * halo + seq), :] = jnp.zeros((2 * halo, LANES), F32)
        tail = slice(seq, seq + pad)
        af_scr[sl, tail, :] = jnp.ones((pad, LANES), F32)
        ab_scr[sl, tail, :] = jnp.ones((pad, LANES), F32)
        bf_scr[sl, tail, :] = jnp.zeros((pad, LANES), F32)
        bb_scr[sl, tail, :] = jnp.zeros((pad, LANES), F32)

    def project(bi, c):
        rows = slice(c * chunk, (c + 1) * chunk)
        hc = h_scr[bi, rows, :]
        ux = _dot(hc, wx_ref[...])
        for s in range(n_lane_grp):
            xb_scr[bi * n_lane_grp + s, xb_rows(c * chunk, chunk), :] = ux[:, s * LANES:(s + 1) * LANES]
        yb_scr[bi, rows, :] = _gelu_tanh_of_half(_dot(hc, wy_ref[...]))

    cw = cw_ref[...]
    lam = lam_ref[...]
    c1_all = (-0.5 * RG_C * LOG2_E) * (jnp.maximum(-lam, 0.0) + jnp.log1p(jnp.exp(-jnp.abs(lam))))
    lane = lax.broadcasted_iota(jnp.int32, (chunk, LANES), 1)
    bias_cols = jnp.where(lane < 2, 1.0, 0.0).astype(BF16)

    def gates(bi, c):
        rows = slice(c * chunk, (c + 1) * chunk)
        for s in range(n_lane_grp):
            sl = bi * n_lane_grp + s
            lanes = slice(s * LANES, (s + 1) * LANES)
            xs = cb_ref[:, lanes] + cw[0:1, lanes] * xb_scr[sl, xb_rows(c * chunk - 1, chunk), :]
            for k in range(1, 4):
                xs = xs + cw[k:k + 1, lanes] * xb_scr[sl, xb_rows(c * chunk - 1 + k, chunk), :]
            g = _dot(jnp.concatenate([xs.astype(BF16), bias_cols], axis=1), wg_ref[s])
            xh = 0.5 * xs
            for d, (a_scr, b_scr) in enumerate(((af_scr, bf_scr), (ab_scr, bb_scr))):
                c1 = c1_all[d:d + 1, lanes]
                tr = jnp.tanh(g[:, d * LANES:(d + 1) * LANES])
                ti = jnp.tanh(g[:, (2 + d) * LANES:(3 + d) * LANES])
                a = jnp.exp2(c1 * tr + c1)
                a_scr[sl, rows, :] = a
                v = 1.0 - a * a
                root = jnp.where(v > 0.0, v * lax.rsqrt(v), 0.0)
                b_scr[sl, rows, :] = root * (ti * xh + xh)

    for bi in range(nb):
        project(bi, 0)
        for c in range(1, n_chunk):
            project(bi, c)
            gates(bi, c - 1)
        gates(bi, n_chunk - 1)

    def seg_rows(j):
        return pl.ds(j, SUBLANES, stride=seg)

    def run_steps(step, init):
        unroll = 8
        main = seg // unroll

        def body(i, carry):
            for u in range(unroll):
                carry = step(i * unroll + u, carry)
            return carry

        carry = lax.fori_loop(0, main, body, init)
        for j in range(main * unroll, seg):
            carry = step(j, carry)
        return carry

    def pass1(j, carry):
        jb = seg - 1 - j
        out = []
        for sl in range(n_slab):
            pf, hf, pb, hb = carry[sl]
            a = af_scr[sl, seg_rows(j), :]
            b = bf_scr[sl, seg_rows(j), :]
            a2 = ab_scr[sl, seg_rows(jb), :]
            b2 = bb_scr[sl, seg_rows(jb), :]
            out.append((pf * a, a * hf + b, pb * a2, a2 * hb + b2))
        return tuple(out)

    one = jnp.ones((SUBLANES, LANES), F32)
    zero = jnp.zeros((SUBLANES, LANES), F32)
    totals = run_steps(pass1, tuple((one, zero, one, zero) for _ in range(n_slab)))

    rid = lax.broadcasted_iota(jnp.int32, (SUBLANES, LANES), 0)
    init_f, init_b = [], []
    for sl in range(n_slab):
        bi, s = divmod(sl, n_lane_grp)
        lanes = slice(s * LANES, (s + 1) * LANES)
        pf, hf, pb, hb = totals[sl]
        st = h0_ref[bi, 0:1, lanes]
        ini = zero
        for k in range(SUBLANES):
            ini = jnp.where(rid == k, st, ini)
            st = hf[k:k + 1] + pf[k:k + 1] * st
        init_f.append(ini)
        fin_ref[bi, 0:1, lanes] = st
        st = h0_ref[bi, 1:2, lanes]
        ini = zero
        for k in range(SUBLANES - 1, -1, -1):
            ini = jnp.where(rid == k, st, ini)
            st = hb[k:k + 1] + pb[k:k + 1] * st
        init_b.append(ini)
        fin_ref[bi, 1:2, lanes] = st

    def pass2(j, carry):
        jb = seg - 1 - j
        out = []
        for sl in range(n_slab):
            hf, hb = carry[sl]
            hf = af_scr[sl, seg_rows(j), :] * hf + bf_scr[sl, seg_rows(j), :]
            hf_scr[sl, seg_rows(j), :] = hf
            hb = ab_scr[sl, seg_rows(jb), :] * hb + bb_scr[sl, seg_rows(jb), :]
            hb_scr[sl, seg_rows(jb), :] = hb
            out.append((hf, hb))
        return tuple(out)

    run_steps(pass2, tuple(zip(init_f, init_b)))

    for sl in range(n_slab):
        bi, s = divmod(sl, n_lane_grp)
        lanes = slice(s * LANES, (s + 1) * LANES)
        for c in range(n_chunk):
            rows = slice(c * chunk, (c + 1) * chunk)
            y_ref[bi, rows, lanes] = (
                (hf_scr[sl, rows, :] + hb_scr[sl, rows, :]) * yb_scr[bi, rows, lanes]).astype(BF16)


def _odd_segment_length(seq):
    seg = -(-seq // SUBLANES)
    return seg if seg % 2 == 1 else seg + 1


def _rg_call(x, mod, layer, row_of_group, gn, w_in, w_index, conv_w, conv_b, wg, lam, h0, *, cblk, nb):
    b, seq, d = x.shape
    d_rnn = lam.shape[1]
    ncb = d_rnn // cblk
    n_lane_grp = cblk // LANES
    n_slab = nb * n_lane_grp
    seg = _odd_segment_length(seq)
    seq_pad = SUBLANES * seg
    chunk = min(seq, 512)
    slab = pltpu.VMEM((n_slab, seq_pad, LANES), F32)
    return pl.pallas_call(
        functools.partial(_rg_kernel, seq=seq, seg=seg, chunk=chunk),
        grid=(b // nb, ncb),
        in_specs=[
            pl.BlockSpec((nb, seq, d), lambda i, c: (i, 0, 0)),
            pl.BlockSpec((1, 1, 6, d), lambda i, c: (layer, row_of_group(i), 0, 0)),
            pl.BlockSpec((1, d), lambda i, c: (0, 0)),
            pl.BlockSpec((None, d, cblk), lambda i, c: (w_index, 0, c)),
            pl.BlockSpec((None, d, cblk), lambda i, c: (w_index, 0, ncb + c)),
            pl.BlockSpec((4, cblk), lambda i, c: (0, c)),
            pl.BlockSpec((1, cblk), lambda i, c: (0, c)),
            pl.BlockSpec((n_lane_grp, 2 * RG_BS, 4 * RG_BS), lambda i, c: (c, 0, 0)),
            pl.BlockSpec((2, cblk), lambda i, c: (0, c)),
            pl.BlockSpec((nb, 2, cblk), lambda i, c: (i, 0, c)),
        ],
        out_specs=[
            pl.BlockSpec((nb, seq, cblk), lambda i, c: (i, 0, c)),
            pl.BlockSpec((nb, 2, cblk), lambda i, c: (i, 0, c)),
        ],
        out_shape=[
            jax.ShapeDtypeStruct((b, seq, d_rnn), BF16),
            jax.ShapeDtypeStruct((b, 2, d_rnn), F32),
        ],
        scratch_shapes=[
            pltpu.VMEM((nb, seq, d), BF16),
            pltpu.VMEM((n_slab, 2 * (seq + 2 * SUBLANES), LANES), F32),
            pltpu.VMEM((nb, seq, cblk), F32),
            slab, slab, slab, slab,
            slab, slab,
        ],
        compiler_params=pltpu.CompilerParams(
            dimension_semantics=("arbitrary", "arbitrary"), vmem_limit_bytes=VMEM_LIMIT),
        name="rg_core",
    )(x, mod, gn, w_in, w_in, conv_w, conv_b, wg, lam, h0)


def _rope_pairs(v, tab):
    p = v * tab
    return p + pltpu.roll(p, QK_ROPE, axis=1)


def _mlaproj_kernel(x_ref, mod_ref, gn_ref, wd_ref, gq_ref, gkv_ref, wuq_ref, wk_ref, wv_ref, tab_ref,
                    tabt_ref, q_ref, k_ref, v_ref, *cache_refs, q_scale):
    mod = mod_ref[0, 0]
    h = _modnorm(x_ref[...], gn_ref[...], mod[0:1], mod[1:2]).astype(BF16)
    proj = _dot(h, wd_ref[...])
    cq = _rms(proj[:, :Q_LORA], gq_ref[...])
    ckv = _rms(proj[:, Q_LORA:Q_LORA + KV_LORA], gkv_ref[...])
    if cache_refs:
        ckv_ref, kpe_ref = cache_refs
        ckv_ref[...] = ckv
        kpe_ref[...] = proj[:, Q_LORA + KV_LORA:Q_LORA + KV_LORA + QK_ROPE]
    kp2 = proj[:, Q_LORA + KV_LORA + LANES:]
    lane = lax.broadcasted_iota(jnp.int32, kp2.shape, 1)
    kr = jnp.where(lane < QK_ROPE, _rope_pairs(kp2, tab_ref[...]), 0.0).astype(BF16)
    qt = _dot_nt(wuq_ref[...], cq.astype(BF16))
    ckvb = ckv.astype(BF16)
    kn = _dot(ckvb, wk_ref[...])
    vt = _dot_nt(wv_ref[...], ckvb).astype(BF16)
    nbt, _, tb = q_ref.shape
    for bi in range(nbt):
        v_ref[bi] = vt[:, bi * tb:(bi + 1) * tb]
    half = QK_ROPE // 2
    cos = tabt_ref[0:half, :] * q_scale
    sin = tabt_ref[half:, :] * q_scale
    zeros = jnp.zeros((HEAD_SLOT - QK_NOPE - QK_ROPE, tb), BF16)
    for hd in range(N_HEADS):
        lo = hd * HEAD_SLOT
        src = hd * (QK_NOPE + QK_ROPE)
        qn = (qt[src:src + QK_NOPE] * q_scale).astype(BF16)
        qe = qt[src + QK_NOPE:src + QK_NOPE + half]
        qo = qt[src + QK_NOPE + half:src + QK_NOPE + QK_ROPE]
        re = (qe * cos - qo * sin).astype(BF16)
        ro = (qe * sin + qo * cos).astype(BF16)
        for bi in range(nbt):
            cols = slice(bi * tb, (bi + 1) * tb)
            q_ref[bi, lo:lo + QK_NOPE, :] = qn[:, cols]
            q_ref[bi, lo + QK_NOPE:lo + QK_NOPE + half, :] = re[:, cols]
            q_ref[bi, lo + QK_NOPE + half:lo + QK_NOPE + QK_ROPE, :] = ro[:, cols]
            q_ref[bi, lo + QK_NOPE + QK_ROPE:lo + HEAD_SLOT, :] = zeros
        k_ref[:, lo:lo + QK_NOPE] = kn[:, hd * QK_NOPE:(hd + 1) * QK_NOPE].astype(BF16)
        k_ref[:, lo + QK_NOPE:lo + HEAD_SLOT] = kr


def _mlaproj_call(x, mod, layer, row_of_tile, gn, wd, gq, gkv, wuq_t, wk, wv_t, tab, tab_t, tab_tile, *,
                  seq, tm, emit_cache):
    ntok, d = x.shape
    const = dict(pipeline_mode=pl.Buffered(1))
    if tm <= seq:
        nt = seq // tm
        fm_block, fm_index = (1, tm), (lambda i: (i // nt, 0, i % nt))
    else:
        fm_block, fm_index = (tm // seq, seq), (lambda i: (i, 0, 0))
    out_specs = [
        pl.BlockSpec((fm_block[0], N_HEADS * HEAD_SLOT, fm_block[1]), fm_index),
        pl.BlockSpec((tm, N_HEADS * HEAD_SLOT), lambda i: (i, 0)),
        pl.BlockSpec((fm_block[0], N_HEADS * V_DIM, fm_block[1]), fm_index),
    ]
    out_shape = [
        jax.ShapeDtypeStruct((ntok // seq, N_HEADS * HEAD_SLOT, seq), BF16),
        jax.ShapeDtypeStruct((ntok, N_HEADS * HEAD_SLOT), BF16),
        jax.ShapeDtypeStruct((ntok // seq, N_HEADS * V_DIM, seq), BF16),
    ]
    if emit_cache:
        out_specs += [pl.BlockSpec((tm, KV_LORA), lambda i: (i, 0)),
                      pl.BlockSpec((tm, QK_ROPE), lambda i: (i, 0))]
        out_shape += [jax.ShapeDtypeStruct((ntok, KV_LORA), F32),
                      jax.ShapeDtypeStruct((ntok, QK_ROPE), F32)]
    return pl.pallas_call(
        functools.partial(_mlaproj_kernel, q_scale=float((QK_NOPE + QK_ROPE) ** -0.5 * LOG2_E)),
        grid=(ntok // tm,),
        in_specs=[
            pl.BlockSpec((tm, d), lambda i: (i, 0)),
            pl.BlockSpec((1, 1, 6, d), lambda i: (layer, row_of_tile(i), 0, 0)),
            pl.BlockSpec((1, d), lambda i: (0, 0)),
            pl.BlockSpec(wd.shape, lambda i: (0, 0), **const),
            pl.BlockSpec((1, Q_LORA), lambda i: (0, 0)),
            pl.BlockSpec((1, KV_LORA), lambda i: (0, 0)),
            pl.BlockSpec(wuq_t.shape, lambda i: (0, 0), **const),
            pl.BlockSpec(wk.shape, lambda i: (0, 0), **const),
            pl.BlockSpec(wv_t.shape, lambda i: (0, 0), **const),
            pl.BlockSpec((tm, LANES), lambda i: (tab_tile(i), 0)),
            pl.BlockSpec((QK_ROPE, tm), lambda i: (0, tab_tile(i))),
        ],
        out_specs=out_specs,
        out_shape=out_shape,
        compiler_params=pltpu.CompilerParams(
            dimension_semantics=("arbitrary",), vmem_limit_bytes=VMEM_LIMIT),
        name="mla_proj",
    )(x, mod, gn, wd, gq, gkv, wuq_t, wk, wv_t, tab, tab_t)


def _cachekv_kernel(ckv_ref, kpe_ref, wk_ref, wv_ref, k_ref, v_ref):
    ckvb = ckv_ref[...].astype(BF16)
    kn = _dot(ckvb, wk_ref[...])
    v_ref[0] = _dot_nt(wv_ref[...], ckvb).astype(BF16)
    kp = kpe_ref[...].astype(BF16)
    for hd in range(N_HEADS):
        lo = hd * HEAD_SLOT
        k_ref[:, lo:lo + QK_NOPE] = kn[:, hd * QK_NOPE:(hd + 1) * QK_NOPE].astype(BF16)
        k_ref[:, lo + QK_NOPE:lo + HEAD_SLOT] = kp


def _cachekv_call(ckv, kpe_padded, wk, wv_t, *, past):
    ntok = ckv.shape[0]
    return pl.pallas_call(
        _cachekv_kernel,
        grid=(ntok // past,),
        in_specs=[
            pl.BlockSpec((past, KV_LORA), lambda i: (i, 0)),
            pl.BlockSpec((past, LANES), lambda i: (i, 0)),
            pl.BlockSpec(wk.shape, lambda i: (0, 0)),
            pl.BlockSpec(wv_t.shape, lambda i: (0, 0)),
        ],
        out_specs=[
            pl.BlockSpec((past, N_HEADS * HEAD_SLOT), lambda i: (i, 0)),
            pl.BlockSpec((1, N_HEADS * V_DIM, past), lambda i: (i, 0, 0)),
        ],
        out_shape=[
            jax.ShapeDtypeStruct((ntok, N_HEADS * HEAD_SLOT), BF16),
            jax.ShapeDtypeStruct((ntok // past, N_HEADS * V_DIM, past), BF16),
        ],
        compiler_params=pltpu.CompilerParams(
            dimension_semantics=("arbitrary",), vmem_limit_bytes=VMEM_LIMIT),
        name="mla_cache_kv",
    )(ckv, kpe_padded, wk, wv_t)


def _attn_kernel(q_ref, k_ref, v_ref, *rest):
    if len(rest) == 4:
        kc_ref, vc_ref, o_ref, s_scr = rest
    else:
        kc_ref = vc_ref = None
        o_ref, s_scr = rest
    kchunk = min(512, k_ref.shape[1])
    chunks = [(k_ref, v_ref, r, r, kchunk) for r in range(0, k_ref.shape[1], kchunk)]
    if kc_ref is not None:
        cchunk = min(kchunk, kc_ref.shape[1])
        chunks += [(kc_ref, vc_ref, r, k_ref.shape[1] + r, cchunk) for r in range(0, kc_ref.shape[1], cchunk)]

    tq = s_scr.shape[2]
    n_sub = q_ref.shape[2] // tq
    n_item = q_ref.shape[0] * n_sub * N_HEADS

    def item_index(it):
        return it // (n_sub * N_HEADS), (it // N_HEADS) % n_sub, it % N_HEADS

    def score_chunk(it, chunk):
        kref, _, r, srow, n = chunk
        bi, sub, hd = item_index(it)
        slot = slice(hd * HEAD_SLOT, (hd + 1) * HEAD_SLOT)
        s = _dot(kref[bi, r:r + n, slot], q_ref[bi, slot, sub * tq:(sub + 1) * tq])
        s_scr[it % n_buf, srow:srow + n, :] = s
        return jnp.max(s, axis=0, keepdims=True)

    def value_chunk(it, chunk, m):
        _, vref, r, srow, n = chunk
        bi, _, hd = item_index(it)
        p = jnp.exp2(s_scr[it % n_buf, srow:srow + n, :] - m)
        ov = _dot(vref[bi, hd * V_DIM:(hd + 1) * V_DIM, r:r + n], p.astype(BF16))
        return jnp.sum(p, axis=0, keepdims=True), ov

    def running_max(acc, new):
        return new if acc is None else jnp.maximum(acc, new)

    n_buf = s_scr.shape[0]
    nc = len(chunks)
    ahead = nc + min(nc, 3)
    assert n_buf == 3 and nc < ahead <= 2 * nc
    total = n_item * nc
    maxes = {}

    def issue_scores(i):
        it, c = divmod(i, nc)
        maxes[it] = running_max(maxes.get(it), score_chunk(it, chunks[c]))

    for i in range(ahead):
        issue_scores(i)
    l, ov = None, None
    for j in range(total):
        if j + ahead < total:
            issue_scores(j + ahead)
        it, c = divmod(j, nc)
        l_c, ov_c = value_chunk(it, chunks[c], maxes[it])
        l = l_c if c == 0 else l + l_c
        ov = ov_c if c == 0 else ov + ov_c
        if c == nc - 1:
            bi, sub, hd = item_index(it)
            row0 = (bi * n_sub + sub) * tq
            o_ref[row0:row0 + tq, hd * V_DIM:(hd + 1) * V_DIM] = (ov * (1.0 / l)).T.astype(BF16)


def _attn_call(q_t, k, v_t, kc, vc_t, *, tq, nsub, nb):
    batch, kw, seq = q_t.shape
    ntok = batch * seq
    tstep = tq * nsub
    nq = seq // tstep
    assert nb == 1 or nq == 1
    vw = N_HEADS * V_DIM
    args = [q_t, k.reshape(batch, seq, kw), v_t]
    in_specs = [
        pl.BlockSpec((nb, kw, tstep), lambda b, i: (b, 0, i)),
        pl.BlockSpec((nb, seq, kw), lambda b, i: (b, 0, 0)),
        pl.BlockSpec((nb, vw, seq), lambda b, i: (b, 0, 0)),
    ]
    past = 0
    if kc is not None:
        past = kc.shape[0] // batch
        args += [kc.reshape(batch, past, kw), vc_t]
        in_specs += [pl.BlockSpec((nb, past, kw), lambda b, i: (b, 0, 0)),
                     pl.BlockSpec((nb, vw, past), lambda b, i: (b, 0, 0))]
    return pl.pallas_call(
        _attn_kernel,
        grid=(batch // nb, nq),
        in_specs=in_specs,
        out_specs=pl.BlockSpec((nb * tstep, vw), lambda b, i: (b * nq + i, 0)),
        out_shape=jax.ShapeDtypeStruct((ntok, vw), BF16),
        scratch_shapes=[pltpu.VMEM((3, seq + past, tq), F32)],
        compiler_params=pltpu.CompilerParams(
            dimension_semantics=("arbitrary", "arbitrary"), vmem_limit_bytes=VMEM_LIMIT),
        name="mla_attn",
    )(*args)


def _deinterleave(w):
    return jnp.concatenate([w[..., 0::2], w[..., 1::2]], axis=-1)


def _deinterleaved_partner(w):
    return jnp.concatenate([-w[..., 1::2], w[..., 0::2]], axis=-1)


def _rope_tables(n):
    rows = n // GRID_W
    row = jnp.repeat(jnp.arange(rows, dtype=F32), GRID_W)
    col = jnp.tile(jnp.arange(GRID_W, dtype=F32), rows)
    half = QK_ROPE // 2
    inv = ROPE_THETA ** (-jnp.arange(0, half, 2, dtype=F32) / half)
    ang = jnp.concatenate([row[:, None] * inv, col[:, None] * inv], axis=-1)
    cos, sin = jnp.cos(ang), jnp.sin(ang)
    return jnp.concatenate([cos, cos, sin, sin], axis=-1), jnp.concatenate([cos, sin], axis=-1).T


def _identity_rope_tables(n):
    half = QK_ROPE // 2
    one, zero = jnp.ones((n, half), F32), jnp.zeros((n, half), F32)
    return jnp.concatenate([one, one, zero, zero], axis=-1), jnp.concatenate([one, zero], axis=-1).T


def kernel(x_prompt, x_sample, state_rglru, cache_ckv, cache_kpe, c, c_ctx, ada_w, ada_b, norm_mix, norm_mlp, mlp_w1, mlp_w2, rg_w_in, rg_conv_w, rg_conv_b, rg_wa, rg_ba, rg_wx, rg_bx, rg_lambda, rg_w_out, mla_w_dqkv, mla_norm_q, mla_norm_kv, mla_w_uq, mla_w_ukv, mla_w_o, final_norm):
    batch, seq, d = x_prompt.shape
    dec_batch, dec_seq, _ = x_sample.shape
    depth = ada_w.shape[0]
    d_rnn = rg_lambda.shape[-1]
    past = cache_ckv.shape[2]
    assert dec_batch == COND_ROWS - SAMPLE_ROW0

    cond = jnp.concatenate(
        [jnp.broadcast_to(c_ctx[None, :], (SAMPLE_ROW0, d)), c], axis=0)
    mod = _ada_call(cond, ada_w, ada_b).reshape(depth, COND_ROWS, 6, d)

    w1 = mlp_w1.astype(BF16)
    w2 = mlp_w2.astype(BF16)
    w_in = jnp.concatenate([rg_w_in[..., :d_rnn], 0.5 * rg_w_in[..., d_rnn:]], axis=-1).astype(BF16)
    w_out = rg_w_out.astype(BF16)
    w_o = mla_w_o.astype(BF16)
    n_rg, n_mla = rg_w_in.shape[0], mla_w_dqkv.shape[0]
    nblk = d_rnn // RG_BS
    wg = (0.5 * jnp.concatenate([rg_wa[:, 0], rg_wa[:, 1], rg_wx[:, 0], rg_wx[:, 1]], axis=-1)).astype(BF16)
    bg = 0.5 * jnp.concatenate(
        [rg_ba[:, 0].reshape(n_rg, nblk, 1, RG_BS), rg_ba[:, 1].reshape(n_rg, nblk, 1, RG_BS),
         rg_bx[:, 0].reshape(n_rg, nblk, 1, RG_BS), rg_bx[:, 1].reshape(n_rg, nblk, 1, RG_BS)], axis=-1)
    bg_hi = bg.astype(BF16)
    bg_lo = (bg - bg_hi.astype(F32)).astype(BF16)
    wg = jnp.concatenate(
        [wg, bg_hi, bg_lo, jnp.zeros((n_rg, nblk, RG_BS - 2, 4 * RG_BS), BF16)], axis=2)
    lo = Q_LORA + KV_LORA
    wd, wuq, wk, wv = [], [], [], []
    for j in range(n_mla):
        w = mla_w_dqkv[j]
        w_pe = w[:, lo:]
        wd.append(jnp.concatenate(
            [w, jnp.zeros((d, LANES - QK_ROPE), F32), _deinterleave(w_pe), _deinterleaved_partner(w_pe)],
            axis=1).astype(BF16))
        wq = mla_w_uq[j].reshape(Q_LORA, N_HEADS, QK_NOPE + QK_ROPE)
        wq = jnp.concatenate([wq[..., :QK_NOPE], _deinterleave(wq[..., QK_NOPE:])], axis=-1)
        wuq.append(wq.reshape(Q_LORA, N_HEADS * (QK_NOPE + QK_ROPE)).T.astype(BF16))
        wkv = mla_w_ukv[j].reshape(KV_LORA, N_HEADS, QK_NOPE + V_DIM)
        wk.append(wkv[..., :QK_NOPE].reshape(KV_LORA, N_HEADS * QK_NOPE).astype(BF16))
        wv.append(wkv[..., QK_NOPE:].reshape(KV_LORA, N_HEADS * V_DIM).T.astype(BF16))

    fn = final_norm.reshape(1, d)
    tabs_sample = _rope_tables(dec_seq)
    tm = 512

    def run_stream(x, h0_all, cache, *, sample):
        b, t, _ = x.shape
        assert (not sample) or t % tm == 0
        tmp = tm
        rg_group = 1 if sample else 4
        if sample:
            row_of_batch = lambda i: SAMPLE_ROW0 + i
            row_of_tile = lambda i: SAMPLE_ROW0 + i // (t // tm)
            row_of_ptile = lambda i: SAMPLE_ROW0 + i // (t // tmp)
            (tab, tab_t), tab_tile = tabs_sample, (lambda i: i % (t // tmp))
        else:
            row_of_batch = lambda i: 0
            row_of_tile = lambda i: 0
            row_of_ptile = lambda i: 0
            (tab, tab_t), tab_tile = _identity_rope_tables(tmp), (lambda i: 0)
        xt = x.reshape(b * t, d)
        states, ckvs, kpes = [], [], []
        for l in range(depth):
            j = l // 2
            if l % 2 == 0:
                y, fin = _rg_call(xt.reshape(b, t, d), mod, l, row_of_batch, norm_mix[l].reshape(1, d),
                                  w_in, j, rg_conv_w[j], rg_conv_b[j].reshape(1, d_rnn), wg[j],
                                  rg_lambda[j], h0_all[:, j], cblk=256, nb=rg_group)
                states.append(fin)
                y = y.reshape(b * t, d_rnn)
                wp = w_out
            else:
                outs = _mlaproj_call(xt, mod, l, row_of_ptile, norm_mix[l].reshape(1, d), wd[j],
                                     mla_norm_q[j].reshape(1, Q_LORA), mla_norm_kv[j].reshape(1, KV_LORA),
                                     wuq[j], wk[j], wv[j], tab, tab_t, tab_tile, seq=t, tm=tmp,
                                     emit_cache=not sample)
                q, k, v = outs[:3]
                if sample:
                    ckv_c, kpe_c = cache
                    kpe_p = jnp.pad(_deinterleave(kpe_c[:, j]).reshape(b * past, QK_ROPE),
                                    ((0, 0), (0, LANES - QK_ROPE)))
                    kc, vc = _cachekv_call(ckv_c[:, j].reshape(b * past, KV_LORA), kpe_p, wk[j], wv[j], past=past)
                else:
                    kc = vc = None
                    ckvs.append(outs[3].reshape(b, t, KV_LORA))
                    kpes.append(outs[4].reshape(b, t, QK_ROPE))
                y = _attn_call(q, k, v, kc, vc, tq=min(512, t), nsub=1, nb=1 if sample else 4)
                wp = w_o
            xt = _post_call(xt, y, mod, l, row_of_tile, norm_mlp[l].reshape(1, d), wp, j, w1, w2, fn,
                            tm=tm, final=(l == depth - 1))
        return xt.reshape(b, t, d), states, ckvs, kpes

    zeros_h0 = jnp.zeros((batch, n_rg, 2, d_rnn), F32)
    y_prompt, states, ckvs, kpes = run_stream(x_prompt, zeros_h0, None, sample=False)
    y_sample, _, _, _ = run_stream(x_sample, state_rglru, (cache_ckv, cache_kpe), sample=True)
    return (y_prompt, y_sample, jnp.stack(states, axis=1), jnp.stack(ckvs, axis=1), jnp.stack(kpes, axis=1))
```

```python
import functools

import jax
import jax.numpy as jnp
from jax import lax
from jax.experimental import pallas as pl
from jax.experimental.pallas import tpu as pltpu

F32 = jnp.float32
BF16 = jnp.bfloat16

EPS = 1e-6
LOG2_E = 1.4426950408889634
RG_C = 8.0
RG_BS = 128
N_HEADS = 8
QK_NOPE = 128
QK_ROPE = 64
V_DIM = 128
Q_LORA = 512
KV_LORA = 256
GRID_W = 64
ROPE_THETA = 10000.0
HEAD_SLOT = 256

LANES = 128
SUBLANES = 8
VMEM_LIMIT = 56 * 1024 * 1024

COND_ROWS = 16
SAMPLE_ROW0 = 8


def _dot(a, b):
    return jnp.dot(a, b, preferred_element_type=F32)


def _dot_nt(a, b):
    return lax.dot_general(a, b, (((1,), (1,)), ((), ())), preferred_element_type=F32)


def _rms(x, g):
    ms = jnp.mean(x * x, axis=-1, keepdims=True)
    return x * lax.rsqrt(ms + EPS) * g


def _modnorm(x, g, shift, scale):
    ms = jnp.mean(x * x, axis=-1, keepdims=True)
    return x * lax.rsqrt(ms + EPS) * (g * (1.0 + scale)) + shift


def _sigmoid(x):
    return 0.5 * jnp.tanh(0.5 * x) + 0.5


def _gelu_tanh_of_half(xh):
    c = 0.7978845608028654
    return xh + xh * jnp.tanh(xh * (2.0 * c + (8.0 * c * 0.044715) * (xh * xh)))


def _ada_kernel(cond_ref, w_ref, b_ref, o_ref):
    c = cond_ref[...]
    s = c * _sigmoid(c)
    rows = s.shape[0]
    s_hi = s.astype(BF16)
    s_lo = (s - s_hi.astype(F32)).astype(BF16)
    w = w_ref[0]
    w_hi = w.astype(BF16)
    w_lo = (w - w_hi.astype(F32)).astype(BF16)
    main = _dot(jnp.concatenate([s_hi, s_lo], axis=0), w_hi)
    o_ref[0] = main[:rows] + main[rows:] + _dot(s_hi, w_lo) + b_ref[0]


def _ada_call(cond, ada_w, ada_b):
    depth, d, n6 = ada_w.shape
    tn = 1536
    return pl.pallas_call(
        _ada_kernel,
        grid=(depth, n6 // tn),
        in_specs=[
            pl.BlockSpec((COND_ROWS, d), lambda l, n: (0, 0)),
            pl.BlockSpec((1, d, tn), lambda l, n: (l, 0, n)),
            pl.BlockSpec((1, 1, tn), lambda l, n: (l, 0, n)),
        ],
        out_specs=pl.BlockSpec((1, COND_ROWS, tn), lambda l, n: (l, 0, n)),
        out_shape=jax.ShapeDtypeStruct((depth, COND_ROWS, n6), F32),
        compiler_params=pltpu.CompilerParams(
            dimension_semantics=("arbitrary", "arbitrary"), vmem_limit_bytes=VMEM_LIMIT),
        name="ada",
    )(cond, ada_w, ada_b.reshape(depth, 1, n6))


def _post_kernel(x_ref, y_ref, mod_ref, gn_ref, wp_ref, w1_ref, w2_ref, fn_ref, o_ref, *,
                 ff_chunk, final):
    mod = mod_ref[0, 0]
    g1, sh2, sc2, g2 = mod[2:3], mod[3:4], mod[4:5], mod[5:6]
    x1 = x_ref[...] + g1 * _dot(y_ref[...], wp_ref[...])
    h = _modnorm(x1, gn_ref[...], sh2, sc2).astype(BF16)
    ff = w1_ref.shape[1]
    acc = jnp.zeros_like(x1)
    for c in range(ff // ff_chunk):
        a = jnp.maximum(_dot(h, w1_ref[:, c * ff_chunk:(c + 1) * ff_chunk]), 0.0)
        acc = acc + _dot((a * a).astype(BF16), w2_ref[c * ff_chunk:(c + 1) * ff_chunk, :])
    out = x1 + g2 * acc
    if final:
        out = _rms(out, fn_ref[...])
    o_ref[...] = out


def _post_call(x, y, mod, layer, row_of_tile, gn, wp, wp_index, w1, w2, fn, *, tm, final):
    ntok, d = x.shape
    ff = w1.shape[2]
    const = dict(pipeline_mode=pl.Buffered(1))
    return pl.pallas_call(
        functools.partial(_post_kernel, ff_chunk=1024, final=final),
        grid=(ntok // tm,),
        in_specs=[
            pl.BlockSpec((tm, d), lambda i: (i, 0)),
            pl.BlockSpec((tm, d), lambda i: (i, 0)),
            pl.BlockSpec((1, 1, 6, d), lambda i: (layer, row_of_tile(i), 0, 0)),
            pl.BlockSpec((1, d), lambda i: (0, 0)),
            pl.BlockSpec((None, d, d), lambda i: (wp_index, 0, 0), **const),
            pl.BlockSpec((None, d, ff), lambda i: (layer, 0, 0), **const),
            pl.BlockSpec((None, ff, d), lambda i: (layer, 0, 0), **const),
            pl.BlockSpec((1, d), lambda i: (0, 0)),
        ],
        out_specs=pl.BlockSpec((tm, d), lambda i: (i, 0)),
        out_shape=jax.ShapeDtypeStruct((ntok, d), F32),
        compiler_params=pltpu.CompilerParams(
            dimension_semantics=("arbitrary",), vmem_limit_bytes=VMEM_LIMIT),
        name="post",
    )(x, y, mod, gn, wp, w1, w2, fn)


def _rg_kernel(x_ref, mod_ref, gn_ref, wx_ref, wy_ref, cw_ref, cb_ref, wg_ref, lam_ref,
               h0_ref, y_ref, fin_ref,
               h_scr, xb_scr, yb_scr, af_scr, bf_scr, ab_scr, bb_scr, hf_scr, hb_scr, *,
               seq, seg, chunk):
    nb = x_ref.shape[0]
    n_lane_grp = wx_ref.shape[1] // LANES
    n_slab = nb * n_lane_grp
    n_chunk = seq // chunk
    pad = SUBLANES * seg - seq
    halo = SUBLANES

    @pl.when(pl.program_id(1) == 0)
    def _():
        mod = mod_ref[0, 0]
        for bi in range(nb):
            for c in range(n_chunk):
                rows = slice(c * chunk, (c + 1) * chunk)
                h_scr[bi, rows, :] = _modnorm(
                    x_ref[bi, rows, :], gn_ref[...], mod[0:1], mod[1:2]).astype(BF16)

    def xb_rows(t0, n):
        return pl.ds(2 * (halo + t0), n, stride=2)

    for sl in range(n_slab):
        xb_scr[sl, 0:2 * halo, :] = jnp.zeros((2 * halo, LANES), F32)
        xb_scr[sl, 2 * (halo + seq):2 * (2 * halo + seq), :] = jnp.zeros((2 * halo, LANES), F32)
        tail = slice(seq, seq + pad)
        af_scr[sl, tail, :] = jnp.ones((pad, LANES), F32)
        ab_scr[sl, tail, :] = jnp.ones((pad, LANES), F32)
        bf_scr[sl, tail, :] = jnp.zeros((pad, LANES), F32)
        bb_scr[sl, tail, :] = jnp.zeros((pad, LANES), F32)

    def project(bi, c):
        rows = slice(c * chunk, (c + 1) * chunk)
        hc = h_scr[bi, rows, :]
        ux = _dot(hc, wx_ref[...])
        for s in range(n_lane_grp):
            xb_scr[bi * n_lane_grp + s, xb_rows(c * chunk, chunk), :] = ux[:, s * LANES:(s + 1) * LANES]
        yb_scr[bi, rows, :] = _gelu_tanh_of_half(_dot(hc, wy_ref[...]))

    cw = cw_ref[...]
    lam = lam_ref[...]
    c1_all = (-0.5 * RG_C * LOG2_E) * (jnp.maximum(-lam, 0.0) + jnp.log1p(jnp.exp(-jnp.abs(lam))))
    lane = lax.broadcasted_iota(jnp.int32, (chunk, LANES), 1)
    bias_cols = jnp.where(lane < 2, 1.0, 0.0).astype(BF16)

    def gates(bi, c):
        rows = slice(c * chunk, (c + 1) * chunk)
        for s in range(n_lane_grp):
            sl = bi * n_lane_grp + s
            lanes = slice(s * LANES, (s + 1) * LANES)
            xs = cb_ref[:, lanes] + cw[0:1, lanes] * xb_scr[sl, xb_rows(c * chunk - 1, chunk), :]
            for k in range(1, 4):
                xs = xs + cw[k:k + 1, lanes] * xb_scr[sl, xb_rows(c * chunk - 1 + k, chunk), :]
            g = _dot(jnp.concatenate([xs.astype(BF16), bias_cols], axis=1), wg_ref[s])
            xh = 0.5 * xs
            for d, (a_scr, b_scr) in enumerate(((af_scr, bf_scr), (ab_scr, bb_scr))):
                c1 = c1_all[d:d + 1, lanes]
                tr = jnp.tanh(g[:, d * LANES:(d + 1) * LANES])
                ti = jnp.tanh(g[:, (2 + d) * LANES:(3 + d) * LANES])
                a = jnp.exp2(c1 * tr + c1)
                a_scr[sl, rows, :] = a
                v = 1.0 - a * a
                root = jnp.where(v > 0.0, v * lax.rsqrt(v), 0.0)
                b_scr[sl, rows, :] = root * (ti * xh + xh)

    for bi in range(nb):
        project(bi, 0)
        for c in range(1, n_chunk):
            project(bi, c)
            gates(bi, c - 1)
        gates(bi, n_chunk - 1)

    def seg_rows(j):
        return pl.ds(j, SUBLANES, stride=seg)

    def run_steps(step, init):
        unroll = 8
        main = seg // unroll

        def body(i, carry):
            for u in range(unroll):
                carry = step(i * unroll + u, carry)
            return carry

        carry = lax.fori_loop(0, main, body, init)
        for j in range(main * unroll, seg):
            carry = step(j, carry)
        return carry

    def pass1(j, carry):
        jb = seg - 1 - j
        out = []
        for sl in range(n_slab):
            pf, hf, pb, hb = carry[sl]
            a = af_scr[sl, seg_rows(j), :]
            b = bf_scr[sl, seg_rows(j), :]
            a2 = ab_scr[sl, seg_rows(jb), :]
            b2 = bb_scr[sl, seg_rows(jb), :]
            out.append((pf * a, a * hf + b, pb * a2, a2 * hb + b2))
        return tuple(out)

    one = jnp.ones((SUBLANES, LANES), F32)
    zero = jnp.zeros((SUBLANES, LANES), F32)
    totals = run_steps(pass1, tuple((one, zero, one, zero) for _ in range(n_slab)))

    rid = lax.broadcasted_iota(jnp.int32, (SUBLANES, LANES), 0)
    init_f, init_b = [], []
    for sl in range(n_slab):
        bi, s = divmod(sl, n_lane_grp)
        lanes = slice(s * LANES, (s + 1) * LANES)
        pf, hf, pb, hb = totals[sl]
        st = h0_ref[bi, 0:1, lanes]
        ini = zero
        for k in range(SUBLANES):
            ini = jnp.where(rid == k, st, ini)
            st = hf[k:k + 1] + pf[k:k + 1] * st
        init_f.append(ini)
        fin_ref[bi, 0:1, lanes] = st
        st = h0_ref[bi, 1:2, lanes]
        ini = zero
        for k in range(SUBLANES - 1, -1, -1):
            ini = jnp.where(rid == k, st, ini)
            st = hb[k:k + 1] + pb[k:k + 1] * st
        init_b.append(ini)
        fin_ref[bi, 1:2, lanes] = st

    def pass2(j, carry):
        jb = seg - 1 - j
        out = []
        for sl in range(n_slab):
            hf, hb = carry[sl]
            hf = af_scr[sl, seg_rows(j), :] * hf + bf_scr[sl, seg_rows(j), :]
            hf_scr[sl, seg_rows(j), :] = hf
            hb = ab_scr[sl, seg_rows(jb), :] * hb + bb_scr[sl, seg_rows(jb), :]
            hb_scr[sl, seg_rows(jb), :] = hb
            out.append((hf, hb))
        return tuple(out)

    run_steps(pass2, tuple(zip(init_f, init_b)))

    for sl in range(n_slab):
        bi, s = divmod(sl, n_lane_grp)
        lanes = slice(s * LANES, (s + 1) * LANES)
        for c in range(n_chunk):
            rows = slice(c * chunk, (c + 1) * chunk)
            y_ref[bi, rows, lanes] = (
                (hf_scr[sl, rows, :] + hb_scr[sl, rows, :]) * yb_scr[bi, rows, lanes]).astype(BF16)


def _odd_segment_length(seq):
    seg = -(-seq // SUBLANES)
    return seg if seg % 2 == 1 else seg + 1


def _rg_call(x, mod, layer, row_of_group, gn, w_in, w_index, conv_w, conv_b, wg, lam, h0, *, cblk, nb):
    b, seq, d = x.shape
    d_rnn = lam.shape[1]
    ncb = d_rnn // cblk
    n_lane_grp = cblk // LANES
    n_slab = nb * n_lane_grp
    seg = _odd_segment_length(seq)
    seq_pad = SUBLANES * seg
    chunk = min(seq, 512)
    slab = pltpu.VMEM((n_slab, seq_pad, LANES), F32)
    return pl.pallas_call(
        functools.partial(_rg_kernel, seq=seq, seg=seg, chunk=chunk),
        grid=(b // nb, ncb),
        in_specs=[
            pl.BlockSpec((nb, seq, d), lambda i, c: (i, 0, 0)),
            pl.BlockSpec((1, 1, 6, d), lambda i, c: (layer, row_of_group(i), 0, 0)),
            pl.BlockSpec((1, d), lambda i, c: (0, 0)),
            pl.BlockSpec((None, d, cblk), lambda i, c: (w_index, 0, c)),
            pl.BlockSpec((None, d, cblk), lambda i, c: (w_index, 0, ncb + c)),
            pl.BlockSpec((4, cblk), lambda i, c: (0, c)),
            pl.BlockSpec((1, cblk), lambda i, c: (0, c)),
            pl.BlockSpec((n_lane_grp, 2 * RG_BS, 4 * RG_BS), lambda i, c: (c, 0, 0)),
            pl.BlockSpec((2, cblk), lambda i, c: (0, c)),
            pl.BlockSpec((nb, 2, cblk), lambda i, c: (i, 0, c)),
        ],
        out_specs=[
            pl.BlockSpec((nb, seq, cblk), lambda i, c: (i, 0, c)),
            pl.BlockSpec((nb, 2, cblk), lambda i, c: (i, 0, c)),
        ],
        out_shape=[
            jax.ShapeDtypeStruct((b, seq, d_rnn), BF16),
            jax.ShapeDtypeStruct((b, 2, d_rnn), F32),
        ],
        scratch_shapes=[
            pltpu.VMEM((nb, seq, d), BF16),
            pltpu.VMEM((n_slab, 2 * (seq + 2 * SUBLANES), LANES), F32),
            pltpu.VMEM((nb, seq, cblk), F32),
            slab, slab, slab, slab,
            slab, slab,
        ],
        compiler_params=pltpu.CompilerParams(
            dimension_semantics=("arbitrary", "arbitrary"), vmem_limit_bytes=VMEM_LIMIT),
        name="rg_core",
    )(x, mod, gn, w_in, w_in, conv_w, conv_b, wg, lam, h0)


def _rope_pairs(v, tab):
    p = v * tab
    return p + pltpu.roll(p, QK_ROPE, axis=1)


def _mlaproj_kernel(x_ref, mod_ref, gn_ref, wd_ref, gq_ref, gkv_ref, wuq_ref, wk_ref, wv_ref, tab_ref,
                    tabt_ref, q_ref, k_ref, v_ref, *cache_refs, q_scale):
    mod = mod_ref[0, 0]
    h = _modnorm(x_ref[...], gn_ref[...], mod[0:1], mod[1:2]).astype(BF16)
    proj = _dot(h, wd_ref[...])
    cq = _rms(proj[:, :Q_LORA], gq_ref[...])
    ckv = _rms(proj[:, Q_LORA:Q_LORA + KV_LORA], gkv_ref[...])
    if cache_refs:
        ckv_ref, kpe_ref = cache_refs
        ckv_ref[...] = ckv
        kpe_ref[...] = proj[:, Q_LORA + KV_LORA:Q_LORA + KV_LORA + QK_ROPE]
    kp2 = proj[:, Q_LORA + KV_LORA + LANES:]
    lane = lax.broadcasted_iota(jnp.int32, kp2.shape, 1)
    kr = jnp.where(lane < QK_ROPE, _rope_pairs(kp2, tab_ref[...]), 0.0).astype(BF16)
    qt = _dot_nt(wuq_ref[...], cq.astype(BF16))
    ckvb = ckv.astype(BF16)
    kn = _dot(ckvb, wk_ref[...])
    vt = _dot_nt(wv_ref[...], ckvb).astype(BF16)
    nbt, _, tb = q_ref.shape
    for bi in range(nbt):
        v_ref[bi] = vt[:, bi * tb:(bi + 1) * tb]
    half = QK_ROPE // 2
    cos = tabt_ref[0:half, :] * q_scale
    sin = tabt_ref[half:, :] * q_scale
    zeros = jnp.zeros((HEAD_SLOT - QK_NOPE - QK_ROPE, tb), BF16)
    for hd in range(N_HEADS):
        lo = hd * HEAD_SLOT
        src = hd * (QK_NOPE + QK_ROPE)
        qn = (qt[src:src + QK_NOPE] * q_scale).astype(BF16)
        qe = qt[src + QK_NOPE:src + QK_NOPE + half]
        qo = qt[src + QK_NOPE + half:src + QK_NOPE + QK_ROPE]
        re = (qe * cos - qo * sin).astype(BF16)
        ro = (qe * sin + qo * cos).astype(BF16)
        for bi in range(nbt):
            cols = slice(bi * tb, (bi + 1) * tb)
            q_ref[bi, lo:lo + QK_NOPE, :] = qn[:, cols]
            q_ref[bi, lo + QK_NOPE:lo + QK_NOPE + half, :] = re[:, cols]
            q_ref[bi, lo + QK_NOPE + half:lo + QK_NOPE + QK_ROPE, :] = ro[:, cols]
            q_ref[bi, lo + QK_NOPE + QK_ROPE:lo + HEAD_SLOT, :] = zeros
        k_ref[:, lo:lo + QK_NOPE] = kn[:, hd * QK_NOPE:(hd + 1) * QK_NOPE].astype(BF16)
        k_ref[:, lo + QK_NOPE:lo + HEAD_SLOT] = kr


def _mlaproj_call(x, mod, layer, row_of_tile, gn, wd, gq, gkv, wuq_t, wk, wv_t, tab, tab_t, tab_tile, *,
                  seq, tm, emit_cache):
    ntok, d = x.shape
    const = dict(pipeline_mode=pl.Buffered(1))
    if tm <= seq:
        nt = seq // tm
        fm_block, fm_index = (1, tm), (lambda i: (i // nt, 0, i % nt))
    else:
        fm_block, fm_index = (tm // seq, seq), (lambda i: (i, 0, 0))
    out_specs = [
        pl.BlockSpec((fm_block[0], N_HEADS * HEAD_SLOT, fm_block[1]), fm_index),
        pl.BlockSpec((tm, N_HEADS * HEAD_SLOT), lambda i: (i, 0)),
        pl.BlockSpec((fm_block[0], N_HEADS * V_DIM, fm_block[1]), fm_index),
    ]
    out_shape = [
        jax.ShapeDtypeStruct((ntok // seq, N_HEADS * HEAD_SLOT, seq), BF16),
        jax.ShapeDtypeStruct((ntok, N_HEADS * HEAD_SLOT), BF16),
        jax.ShapeDtypeStruct((ntok // seq, N_HEADS * V_DIM, seq), BF16),
    ]
    if emit_cache:
        out_specs += [pl.BlockSpec((tm, KV_LORA), lambda i: (i, 0)),
                      pl.BlockSpec((tm, QK_ROPE), lambda i: (i, 0))]
        out_shape += [jax.ShapeDtypeStruct((ntok, KV_LORA), F32),
                      jax.ShapeDtypeStruct((ntok, QK_ROPE), F32)]
    return pl.pallas_call(
        functools.partial(_mlaproj_kernel, q_scale=float((QK_NOPE + QK_ROPE) ** -0.5 * LOG2_E)),
        grid=(ntok // tm,),
        in_specs=[
            pl.BlockSpec((tm, d), lambda i: (i, 0)),
            pl.BlockSpec((1, 1, 6, d), lambda i: (layer, row_of_tile(i), 0, 0)),
            pl.BlockSpec((1, d), lambda i: (0, 0)),
            pl.BlockSpec(wd.shape, lambda i: (0, 0), **const),
            pl.BlockSpec((1, Q_LORA), lambda i: (0, 0)),
            pl.BlockSpec((1, KV_LORA), lambda i: (0, 0)),
            pl.BlockSpec(wuq_t.shape, lambda i: (0, 0), **const),
            pl.BlockSpec(wk.shape, lambda i: (0, 0), **const),
            pl.BlockSpec(wv_t.shape, lambda i: (0, 0), **const),
            pl.BlockSpec((tm, LANES), lambda i: (tab_tile(i), 0)),
            pl.BlockSpec((QK_ROPE, tm), lambda i: (0, tab_tile(i))),
        ],
        out_specs=out_specs,
        out_shape=out_shape,
        compiler_params=pltpu.CompilerParams(
            dimension_semantics=("arbitrary",), vmem_limit_bytes=VMEM_LIMIT),
        name="mla_proj",
    )(x, mod, gn, wd, gq, gkv, wuq_t, wk, wv_t, tab, tab_t)


def _cachekv_kernel(ckv_ref, kpe_ref, wk_ref, wv_ref, k_ref, v_ref):
    ckvb = ckv_ref[...].astype(BF16)
    kn = _dot(ckvb, wk_ref[...])
    v_ref[0] = _dot_nt(wv_ref[...], ckvb).astype(BF16)
    kp = kpe_ref[...].astype(BF16)
    for hd in range(N_HEADS):
        lo = hd * HEAD_SLOT
        k_ref[:, lo:lo + QK_NOPE] = kn[:, hd * QK_NOPE:(hd + 1) * QK_NOPE].astype(BF16)
        k_ref[:, lo + QK_NOPE:lo + HEAD_SLOT] = kp


def _cachekv_call(ckv, kpe_padded, wk, wv_t, *, past):
    ntok = ckv.shape[0]
    return pl.pallas_call(
        _cachekv_kernel,
        grid=(ntok // past,),
        in_specs=[
            pl.BlockSpec((past, KV_LORA), lambda i: (i, 0)),
            pl.BlockSpec((past, LANES), lambda i: (i, 0)),
            pl.BlockSpec(wk.shape, lambda i: (0, 0)),
            pl.BlockSpec(wv_t.shape, lambda i: (0, 0)),
        ],
        out_specs=[
            pl.BlockSpec((past, N_HEADS * HEAD_SLOT), lambda i: (i, 0)),
            pl.BlockSpec((1, N_HEADS * V_DIM, past), lambda i: (i, 0, 0)),
        ],
        out_shape=[
            jax.ShapeDtypeStruct((ntok, N_HEADS * HEAD_SLOT), BF16),
            jax.ShapeDtypeStruct((ntok // past, N_HEADS * V_DIM, past), BF16),
        ],
        compiler_params=pltpu.CompilerParams(
            dimension_semantics=("arbitrary",), vmem_limit_bytes=VMEM_LIMIT),
        name="mla_cache_kv",
    )(ckv, kpe_padded, wk, wv_t)


def _attn_kernel(q_ref, k_ref, v_ref, *rest):
    if len(rest) == 4:
        kc_ref, vc_ref, o_ref, s_scr = rest
    else:
        kc_ref = vc_ref = None
        o_ref, s_scr = rest
    kchunk = min(512, k_ref.shape[1])
    chunks = [(k_ref, v_ref, r, r, kchunk) for r in range(0, k_ref.shape[1], kchunk)]
    if kc_ref is not None:
        cchunk = min(kchunk, kc_ref.shape[1])
        chunks += [(kc_ref, vc_ref, r, k_ref.shape[1] + r, cchunk) for r in range(0, kc_ref.shape[1], cchunk)]

    tq = s_scr.shape[2]
    n_sub = q_ref.shape[2] // tq
    n_item = q_ref.shape[0] * n_sub * N_HEADS

    def item_index(it):
        return it // (n_sub * N_HEADS), (it // N_HEADS) % n_sub, it % N_HEADS

    def score_chunk(it, chunk):
        kref, _, r, srow, n = chunk
        bi, sub, hd = item_index(it)
        slot = slice(hd * HEAD_SLOT, (hd + 1) * HEAD_SLOT)
        s = _dot(kref[bi, r:r + n, slot], q_ref[bi, slot, sub * tq:(sub + 1) * tq])
        s_scr[it % n_buf, srow:srow + n, :] = s
        return jnp.max(s, axis=0, keepdims=True)

    def value_chunk(it, chunk, m):
        _, vref, r, srow, n = chunk
        bi, _, hd = item_index(it)
        p = jnp.exp2(s_scr[it % n_buf, srow:srow + n, :] - m)
        ov = _dot(vref[bi, hd * V_DIM:(hd + 1) * V_DIM, r:r + n], p.astype(BF16))
        return jnp.sum(p, axis=0, keepdims=True), ov

    def running_max(acc, new):
        return new if acc is None else jnp.maximum(acc, new)

    n_buf = s_scr.shape[0]
    nc = len(chunks)
    ahead = nc + min(nc, 3)
    assert n_buf == 3 and nc < ahead <= 2 * nc
    total = n_item * nc
    maxes = {}

    def issue_scores(i):
        it, c = divmod(i, nc)
        maxes[it] = running_max(maxes.get(it), score_chunk(it, chunks[c]))

    for i in range(ahead):
        issue_scores(i)
    l, ov = None, None
    for j in range(total):
        if j + ahead < total:
            issue_scores(j + ahead)
        it, c = divmod(j, nc)
        l_c, ov_c = value_chunk(it, chunks[c], maxes[it])
        l = l_c if c == 0 else l + l_c
        ov = ov_c if c == 0 else ov + ov_c
        if c == nc - 1:
            bi, sub, hd = item_index(it)
            row0 = (bi * n_sub + sub) * tq
            o_ref[row0:row0 + tq, hd * V_DIM:(hd + 1) * V_DIM] = (ov * (1.0 / l)).T.astype(BF16)


def _attn_call(q_t, k, v_t, kc, vc_t, *, tq, nsub, nb):
    batch, kw, seq = q_t.shape
    ntok = batch * seq
    tstep = tq * nsub
    nq = seq // tstep
    assert nb == 1 or nq == 1
    vw = N_HEADS * V_DIM
    args = [q_t, k.reshape(batch, seq, kw), v_t]
    in_specs = [
        pl.BlockSpec((nb, kw, tstep), lambda b, i: (b, 0, i)),
        pl.BlockSpec((nb, seq, kw), lambda b, i: (b, 0, 0)),
        pl.BlockSpec((nb, vw, seq), lambda b, i: (b, 0, 0)),
    ]
    past = 0
    if kc is not None:
        past = kc.shape[0] // batch
        args += [kc.reshape(batch, past, kw), vc_t]
        in_specs += [pl.BlockSpec((nb, past, kw), lambda b, i: (b, 0, 0)),
                     pl.BlockSpec((nb, vw, past), lambda b, i: (b, 0, 0))]
    return pl.pallas_call(
        _attn_kernel,
        grid=(batch // nb, nq),
        in_specs=in_specs,
        out_specs=pl.BlockSpec((nb * tstep, vw), lambda b, i: (b * nq + i, 0)),
        out_shape=jax.ShapeDtypeStruct((ntok, vw), BF16),
        scratch_shapes=[pltpu.VMEM((3, seq + past, tq), F32)],
        compiler_params=pltpu.CompilerParams(
            dimension_semantics=("arbitrary", "arbitrary"), vmem_limit_bytes=VMEM_LIMIT),
        name="mla_attn",
    )(*args)


def _pairs(w):
    wp = jnp.swapaxes(w.reshape(w.shape[:-1] + (w.shape[-1] // 2, 2)), -1, -2)
    return wp[..., 0, :], wp[..., 1, :]


def _deinterleave(w):
    x, y = _pairs(w)
    return jnp.concatenate([x, y], axis=-1)


def _deinterleaved_partner(w):
    x, y = _pairs(w)
    return jnp.concatenate([-y, x], axis=-1)


def _rope_tables(n):
    rows = n // GRID_W
    row = jnp.repeat(jnp.arange(rows, dtype=F32), GRID_W)
    col = jnp.tile(jnp.arange(GRID_W, dtype=F32), rows)
    half = QK_ROPE // 2
    inv = ROPE_THETA ** (-jnp.arange(0, half, 2, dtype=F32) / half)
    ang = jnp.concatenate([row[:, None] * inv, col[:, None] * inv], axis=-1)
    cos, sin = jnp.cos(ang), jnp.sin(ang)
    return jnp.concatenate([cos, cos, sin, sin], axis=-1), jnp.concatenate([cos, sin], axis=-1).T


def _identity_rope_tables(n):
    half = QK_ROPE // 2
    one, zero = jnp.ones((n, half), F32), jnp.zeros((n, half), F32)
    return jnp.concatenate([one, one, zero, zero], axis=-1), jnp.concatenate([one, zero], axis=-1).T


def kernel(x_prompt, x_sample, state_rglru, cache_ckv, cache_kpe, c, c_ctx, ada_w, ada_b, norm_mix, norm_mlp, mlp_w1, mlp_w2, rg_w_in, rg_conv_w, rg_conv_b, rg_wa, rg_ba, rg_wx, rg_bx, rg_lambda, rg_w_out, mla_w_dqkv, mla_norm_q, mla_norm_kv, mla_w_uq, mla_w_ukv, mla_w_o, final_norm):
    batch, seq, d = x_prompt.shape
    dec_batch, dec_seq, _ = x_sample.shape
    depth = ada_w.shape[0]
    d_rnn = rg_lambda.shape[-1]
    past = cache_ckv.shape[2]
    assert dec_batch == COND_ROWS - SAMPLE_ROW0

    cond = jnp.concatenate(
        [jnp.broadcast_to(c_ctx[None, :], (SAMPLE_ROW0, d)), c], axis=0)
    mod = _ada_call(cond, ada_w, ada_b).reshape(depth, COND_ROWS, 6, d)

    w1 = mlp_w1.astype(BF16)
    w2 = mlp_w2.astype(BF16)
    w_in = jnp.concatenate([rg_w_in[..., :d_rnn], 0.5 * rg_w_in[..., d_rnn:]], axis=-1).astype(BF16)
    w_out = rg_w_out.astype(BF16)
    w_o = mla_w_o.astype(BF16)
    n_rg, n_mla = rg_w_in.shape[0], mla_w_dqkv.shape[0]
    nblk = d_rnn // RG_BS
    wg = (0.5 * jnp.concatenate([rg_wa[:, 0], rg_wa[:, 1], rg_wx[:, 0], rg_wx[:, 1]], axis=-1)).astype(BF16)
    bg = 0.5 * jnp.concatenate(
        [rg_ba[:, 0].reshape(n_rg, nblk, 1, RG_BS), rg_ba[:, 1].reshape(n_rg, nblk, 1, RG_BS),
         rg_bx[:, 0].reshape(n_rg, nblk, 1, RG_BS), rg_bx[:, 1].reshape(n_rg, nblk, 1, RG_BS)], axis=-1)
    bg_hi = bg.astype(BF16)
    bg_lo = (bg - bg_hi.astype(F32)).astype(BF16)
    wg = jnp.concatenate(
        [wg, bg_hi, bg_lo, jnp.zeros((n_rg, nblk, RG_BS - 2, 4 * RG_BS), BF16)], axis=2)
    lo = Q_LORA + KV_LORA
    wd, wuq, wk, wv = [], [], [], []
    for j in range(n_mla):
        w = mla_w_dqkv[j]
        w_pe = w[:, lo:]
        wd.append(jnp.concatenate(
            [w, jnp.zeros((d, LANES - QK_ROPE), F32), _deinterleave(w_pe), _deinterleaved_partner(w_pe)],
            axis=1).astype(BF16))
        wq = mla_w_uq[j].reshape(Q_LORA, N_HEADS, QK_NOPE + QK_ROPE)
        wq = jnp.concatenate([wq[..., :QK_NOPE], _deinterleave(wq[..., QK_NOPE:])], axis=-1)
        wuq.append(wq.reshape(Q_LORA, N_HEADS * (QK_NOPE + QK_ROPE)).T.astype(BF16))
        wkv = mla_w_ukv[j].reshape(KV_LORA, N_HEADS, QK_NOPE + V_DIM)
        wk.append(wkv[..., :QK_NOPE].reshape(KV_LORA, N_HEADS * QK_NOPE).astype(BF16))
        wv.append(wkv[..., QK_NOPE:].reshape(KV_LORA, N_HEADS * V_DIM).T.astype(BF16))

    fn = final_norm.reshape(1, d)
    tabs_sample = _rope_tables(dec_seq)
    tm = 512

    def run_stream(x, h0_all, cache, *, sample):
        b, t, _ = x.shape
        assert (not sample) or t % tm == 0
        tmp = tm
        rg_group = 1 if sample else 4
        if sample:
            row_of_batch = lambda i: SAMPLE_ROW0 + i
            row_of_tile = lambda i: SAMPLE_ROW0 + i // (t // tm)
            row_of_ptile = lambda i: SAMPLE_ROW0 + i // (t // tmp)
            (tab, tab_t), tab_tile = tabs_sample, (lambda i: i % (t // tmp))
        else:
            row_of_batch = lambda i: 0
            row_of_tile = lambda i: 0
            row_of_ptile = lambda i: 0
            (tab, tab_t), tab_tile = _identity_rope_tables(tmp), (lambda i: 0)
        xt = x.reshape(b * t, d)
        states, ckvs, kpes = [], [], []
        for l in range(depth):
            j = l // 2
            if l % 2 == 0:
                y, fin = _rg_call(xt.reshape(b, t, d), mod, l, row_of_batch, norm_mix[l].reshape(1, d),
                                  w_in, j, rg_conv_w[j], rg_conv_b[j].reshape(1, d_rnn), wg[j],
                                  rg_lambda[j], h0_all[:, j], cblk=256, nb=rg_group)
                states.append(fin)
                y = y.reshape(b * t, d_rnn)
                wp = w_out
            else:
                outs = _mlaproj_call(xt, mod, l, row_of_ptile, norm_mix[l].reshape(1, d), wd[j],
                                     mla_norm_q[j].reshape(1, Q_LORA), mla_norm_kv[j].reshape(1, KV_LORA),
                                     wuq[j], wk[j], wv[j], tab, tab_t, tab_tile, seq=t, tm=tmp,
                                     emit_cache=not sample)
                q, k, v = outs[:3]
                if sample:
                    ckv_c, kpe_c = cache
                    kpe_p = jnp.pad(_deinterleave(kpe_c[:, j]).reshape(b * past, QK_ROPE),
                                    ((0, 0), (0, LANES - QK_ROPE)))
                    kc, vc = _cachekv_call(ckv_c[:, j].reshape(b * past, KV_LORA), kpe_p, wk[j], wv[j], past=past)
                else:
                    kc = vc = None
                    ckvs.append(outs[3].reshape(b, t, KV_LORA))
                    kpes.append(outs[4].reshape(b, t, QK_ROPE))
                y = _attn_call(q, k, v, kc, vc, tq=min(512, t), nsub=1, nb=1 if sample else 4)
                wp = w_o
            xt = _post_call(xt, y, mod, l, row_of_tile, norm_mlp[l].reshape(1, d), wp, j, w1, w2, fn,
                            tm=tm, final=(l == depth - 1))
        return xt.reshape(b, t, d), states, ckvs, kpes

    zeros_h0 = jnp.zeros((batch, n_rg, 2, d_rnn), F32)
    y_prompt, states, ckvs, kpes = run_stream(x_prompt, zeros_h0, None, sample=False)
    y_sample, _, _, _ = run_stream(x_sample, state_rglru, (cache_ckv, cache_kpe), sample=True)
    return (y_prompt, y_sample, jnp.stack(states, axis=1), jnp.stack(ckvs, axis=1), jnp.stack(kpes, axis=1))
```

```python
import functools

import jax
import jax.numpy as jnp
from jax import lax
from jax.experimental import pallas as pl
from jax.experimental.pallas import tpu as pltpu

F32 = jnp.float32
BF16 = jnp.bfloat16

EPS = 1e-6
LOG2_E = 1.4426950408889634
RG_C = 8.0
RG_BS = 128
N_HEADS = 8
QK_NOPE = 128
QK_ROPE = 64
V_DIM = 128
Q_LORA = 512
KV_LORA = 256
GRID_W = 64
ROPE_THETA = 10000.0
HEAD_SLOT = 256

LANES = 128
SUBLANES = 8
VMEM_LIMIT = 56 * 1024 * 1024

COND_ROWS = 16
SAMPLE_ROW0 = 8


def _dot(a, b):
    return jnp.dot(a, b, preferred_element_type=F32)


def _dot_nt(a, b):
    return lax.dot_general(a, b, (((1,), (1,)), ((), ())), preferred_element_type=F32)


def _rms(x, g):
    ms = jnp.mean(x * x, axis=-1, keepdims=True)
    return x * lax.rsqrt(ms + EPS) * g


def _modnorm(x, g, shift, scale):
    ms = jnp.mean(x * x, axis=-1, keepdims=True)
    return x * lax.rsqrt(ms + EPS) * (g * (1.0 + scale)) + shift


def _sigmoid(x):
    return 0.5 * jnp.tanh(0.5 * x) + 0.5


def _gelu_tanh_of_half(xh):
    c = 0.7978845608028654
    return xh + xh * jnp.tanh(xh * (2.0 * c + (8.0 * c * 0.044715) * (xh * xh)))


def _ada_kernel(cond_ref, w_ref, b_ref, o_ref):
    c = cond_ref[...]
    s = c * _sigmoid(c)
    rows = s.shape[0]
    s_hi = s.astype(BF16)
    s_lo = (s - s_hi.astype(F32)).astype(BF16)
    w = w_ref[0]
    w_hi = w.astype(BF16)
    w_lo = (w - w_hi.astype(F32)).astype(BF16)
    main = _dot(jnp.concatenate([s_hi, s_lo], axis=0), w_hi)
    o_ref[0] = main[:rows] + main[rows:] + _dot(s_hi, w_lo) + b_ref[0]


def _ada_call(cond, ada_w, ada_b):
    depth, d, n6 = ada_w.shape
    tn = 1536
    return pl.pallas_call(
        _ada_kernel,
        grid=(depth, n6 // tn),
        in_specs=[
            pl.BlockSpec((COND_ROWS, d), lambda l, n: (0, 0)),
            pl.BlockSpec((1, d, tn), lambda l, n: (l, 0, n)),
            pl.BlockSpec((1, 1, tn), lambda l, n: (l, 0, n)),
        ],
        out_specs=pl.BlockSpec((1, COND_ROWS, tn), lambda l, n: (l, 0, n)),
        out_shape=jax.ShapeDtypeStruct((depth, COND_ROWS, n6), F32),
        compiler_params=pltpu.CompilerParams(
            dimension_semantics=("arbitrary", "arbitrary"), vmem_limit_bytes=VMEM_LIMIT),
        name="ada",
    )(cond, ada_w, ada_b.reshape(depth, 1, n6))


def _post_kernel(x_ref, y_ref, mod_ref, gn_ref, wp_ref, w1_ref, w2_ref, fn_ref, o_ref, *,
                 ff_chunk, final):
    mod = mod_ref[0, 0]
    g1, sh2, sc2, g2 = mod[2:3], mod[3:4], mod[4:5], mod[5:6]
    x1 = x_ref[...] + g1 * _dot(y_ref[...], wp_ref[...])
    h = _modnorm(x1, gn_ref[...], sh2, sc2).astype(BF16)
    ff = w1_ref.shape[1]
    acc = jnp.zeros_like(x1)
    for c in range(ff // ff_chunk):
        a = jnp.maximum(_dot(h, w1_ref[:, c * ff_chunk:(c + 1) * ff_chunk]), 0.0)
        acc = acc + _dot((a * a).astype(BF16), w2_ref[c * ff_chunk:(c + 1) * ff_chunk, :])
    out = x1 + g2 * acc
    if final:
        out = _rms(out, fn_ref[...])
    o_ref[...] = out


def _post_call(x, y, mod, layer, row_of_tile, gn, wp, wp_index, w1, w2, fn, *, tm, final):
    ntok, d = x.shape
    ff = w1.shape[2]
    const = dict(pipeline_mode=pl.Buffered(1))
    return pl.pallas_call(
        functools.partial(_post_kernel, ff_chunk=1024, final=final),
        grid=(ntok // tm,),
        in_specs=[
            pl.BlockSpec((tm, d), lambda i: (i, 0)),
            pl.BlockSpec((tm, d), lambda i: (i, 0)),
            pl.BlockSpec((1, 1, 6, d), lambda i: (layer, row_of_tile(i), 0, 0)),
            pl.BlockSpec((1, d), lambda i: (0, 0)),
            pl.BlockSpec((None, d, d), lambda i: (wp_index, 0, 0), **const),
            pl.BlockSpec((None, d, ff), lambda i: (layer, 0, 0), **const),
            pl.BlockSpec((None, ff, d), lambda i: (layer, 0, 0), **const),
            pl.BlockSpec((1, d), lambda i: (0, 0)),
        ],
        out_specs=pl.BlockSpec((tm, d), lambda i: (i, 0)),
        out_shape=jax.ShapeDtypeStruct((ntok, d), F32),
        compiler_params=pltpu.CompilerParams(
            dimension_semantics=("arbitrary",), vmem_limit_bytes=VMEM_LIMIT),
        name="post",
    )(x, y, mod, gn, wp, w1, w2, fn)


def _rg_kernel(x_ref, mod_ref, gn_ref, wx_ref, wy_ref, cw_ref, cb_ref, wg_ref, lam_ref,
               h0_ref, y_ref, fin_ref,
               h_scr, xb_scr, yb_scr, af_scr, bf_scr, ab_scr, bb_scr, hf_scr, hb_scr, *,
               seq, seg, chunk):
    nb = x_ref.shape[0]
    n_lane_grp = wx_ref.shape[1] // LANES
    n_slab = nb * n_lane_grp
    n_chunk = seq // chunk
    pad = SUBLANES * seg - seq
    halo = SUBLANES

    @pl.when(pl.program_id(1) == 0)
    def _():
        mod = mod_ref[0, 0]
        for bi in range(nb):
            for c in range(n_chunk):
                rows = slice(c * chunk, (c + 1) * chunk)
                h_scr[bi, rows, :] = _modnorm(
                    x_ref[bi, rows, :], gn_ref[...], mod[0:1], mod[1:2]).astype(BF16)

    def xb_rows(t0, n):
        return pl.ds(2 * (halo + t0), n, stride=2)

    for sl in range(n_slab):
        xb_scr[sl, 0:2 * halo, :] = jnp.zeros((2 * halo, LANES), F32)
        xb_scr[sl, 2 * (halo + seq):2 * (2 * halo + seq), :] = jnp.zeros((2 * halo, LANES), F32)
        tail = slice(seq, seq + pad)
        af_scr[sl, tail, :] = jnp.ones((pad, LANES), F32)
        ab_scr[sl, tail, :] = jnp.ones((pad, LANES), F32)
        bf_scr[sl, tail, :] = jnp.zeros((pad, LANES), F32)
        bb_scr[sl, tail, :] = jnp.zeros((pad, LANES), F32)

    def project(bi, c):
        rows = slice(c * chunk, (c + 1) * chunk)
        hc = h_scr[bi, rows, :]
        ux = _dot(hc, wx_ref[...])
        for s in range(n_lane_grp):
            xb_scr[bi * n_lane_grp + s, xb_rows(c * chunk, chunk), :] = ux[:, s * LANES:(s + 1) * LANES]
        yb_scr[bi, rows, :] = _gelu_tanh_of_half(_dot(hc, wy_ref[...]))

    cw = cw_ref[...]
    lam = lam_ref[...]
    c1_all = (-0.5 * RG_C * LOG2_E) * (jnp.maximum(-lam, 0.0) + jnp.log1p(jnp.exp(-jnp.abs(lam))))
    lane = lax.broadcasted_iota(jnp.int32, (chunk, LANES), 1)
    bias_cols = jnp.where(lane < 2, 1.0, 0.0).astype(BF16)

    def gates(bi, c):
        rows = slice(c * chunk, (c + 1) * chunk)
        for s in range(n_lane_grp):
            sl = bi * n_lane_grp + s
            lanes = slice(s * LANES, (s + 1) * LANES)
            xs = cb_ref[:, lanes] + cw[0:1, lanes] * xb_scr[sl, xb_rows(c * chunk - 1, chunk), :]
            for k in range(1, 4):
                xs = xs + cw[k:k + 1, lanes] * xb_scr[sl, xb_rows(c * chunk - 1 + k, chunk), :]
            g = _dot(jnp.concatenate([xs.astype(BF16), bias_cols], axis=1), wg_ref[s])
            xh = 0.5 * xs
            for d, (a_scr, b_scr) in enumerate(((af_scr, bf_scr), (ab_scr, bb_scr))):
                c1 = c1_all[d:d + 1, lanes]
                tr = jnp.tanh(g[:, d * LANES:(d + 1) * LANES])
                ti = jnp.tanh(g[:, (2 + d) * LANES:(3 + d) * LANES])
                a = jnp.exp2(c1 * tr + c1)
                a_scr[sl, rows, :] = a
                v = 1.0 - a * a
                root = jnp.where(v > 0.0, v * lax.rsqrt(v), 0.0)
                b_scr[sl, rows, :] = root * (ti * xh + xh)

    for bi in range(nb):
        project(bi, 0)
        for c in range(1, n_chunk):
            project(bi, c)
            gates(bi, c - 1)
        gates(bi, n_chunk - 1)

    def seg_rows(j):
        return pl.ds(j, SUBLANES, stride=seg)

    def run_steps(step, init):
        unroll = 8
        main = seg // unroll

        def body(i, carry):
            for u in range(unroll):
                carry = step(i * unroll + u, carry)
            return carry

        carry = lax.fori_loop(0, main, body, init)
        for j in range(main * unroll, seg):
            carry = step(j, carry)
        return carry

    def pass1(j, carry):
        jb = seg - 1 - j
        out = []
        for sl in range(n_slab):
            pf, hf, pb, hb = carry[sl]
            a = af_scr[sl, seg_rows(j), :]
            b = bf_scr[sl, seg_rows(j), :]
            a2 = ab_scr[sl, seg_rows(jb), :]
            b2 = bb_scr[sl, seg_rows(jb), :]
            out.append((pf * a, a * hf + b, pb * a2, a2 * hb + b2))
        return tuple(out)

    one = jnp.ones((SUBLANES, LANES), F32)
    zero = jnp.zeros((SUBLANES, LANES), F32)
    totals = run_steps(pass1, tuple((one, zero, one, zero) for _ in range(n_slab)))

    rid = lax.broadcasted_iota(jnp.int32, (SUBLANES, LANES), 0)
    init_f, init_b = [], []
    for sl in range(n_slab):
        bi, s = divmod(sl, n_lane_grp)
        lanes = slice(s * LANES, (s + 1) * LANES)
        pf, hf, pb, hb = totals[sl]
        st = h0_ref[bi, 0:1, lanes]
        ini = zero
        for k in range(SUBLANES):
            ini = jnp.where(rid == k, st, ini)
            st = hf[k:k + 1] + pf[k:k + 1] * st
        init_f.append(ini)
        fin_ref[bi, 0:1, lanes] = st
        st = h0_ref[bi, 1:2, lanes]
        ini = zero
        for k in range(SUBLANES - 1, -1, -1):
            ini = jnp.where(rid == k, st, ini)
            st = hb[k:k + 1] + pb[k:k + 1] * st
        init_b.append(ini)
        fin_ref[bi, 1:2, lanes] = st

    def pass2(j, carry):
        jb = seg - 1 - j
        out = []
        for sl in range(n_slab):
            hf, hb = carry[sl]
            hf = af_scr[sl, seg_rows(j), :] * hf + bf_scr[sl, seg_rows(j), :]
            hf_scr[sl, seg_rows(j), :] = hf
            hb = ab_scr[sl, seg_rows(jb), :] * hb + bb_scr[sl, seg_rows(jb), :]
            hb_scr[sl, seg_rows(jb), :] = hb
            out.append((hf, hb))
        return tuple(out)

    run_steps(pass2, tuple(zip(init_f, init_b)))

    for sl in range(n_slab):
        bi, s = divmod(sl, n_lane_grp)
        lanes = slice(s * LANES, (s + 1) * LANES)
        for c in range(n_chunk):
            rows = slice(c * chunk, (c + 1) * chunk)
            y_ref[bi, rows, lanes] = (
                (hf_scr[sl, rows, :] + hb_scr[sl, rows, :]) * yb_scr[bi, rows, lanes]).astype(BF16)


def _odd_segment_length(seq):
    seg = -(-seq // SUBLANES)
    return seg if seg % 2 == 1 else seg + 1


def _rg_call(x, mod, layer, row_of_group, gn, w_in, w_index, conv_w, conv_b, wg, lam, h0, *, cblk, nb):
    b, seq, d = x.shape
    d_rnn = lam.shape[1]
    ncb = d_rnn // cblk
    n_lane_grp = cblk // LANES
    n_slab = nb * n_lane_grp
    seg = _odd_segment_length(seq)
    seq_pad = SUBLANES * seg
    chunk = min(256, seq // 2)
    slab = pltpu.VMEM((n_slab, seq_pad, LANES), F32)
    return pl.pallas_call(
        functools.partial(_rg_kernel, seq=seq, seg=seg, chunk=chunk),
        grid=(b // nb, ncb),
        in_specs=[
            pl.BlockSpec((nb, seq, d), lambda i, c: (i, 0, 0)),
            pl.BlockSpec((1, 1, 6, d), lambda i, c: (layer, row_of_group(i), 0, 0)),
            pl.BlockSpec((1, d), lambda i, c: (0, 0)),
            pl.BlockSpec((None, d, cblk), lambda i, c: (w_index, 0, c)),
            pl.BlockSpec((None, d, cblk), lambda i, c: (w_index, 0, ncb + c)),
            pl.BlockSpec((4, cblk), lambda i, c: (0, c)),
            pl.BlockSpec((1, cblk), lambda i, c: (0, c)),
            pl.BlockSpec((n_lane_grp, 2 * RG_BS, 4 * RG_BS), lambda i, c: (c, 0, 0)),
            pl.BlockSpec((2, cblk), lambda i, c: (0, c)),
            pl.BlockSpec((nb, 2, cblk), lambda i, c: (i, 0, c)),
        ],
        out_specs=[
            pl.BlockSpec((nb, seq, cblk), lambda i, c: (i, 0, c)),
            pl.BlockSpec((nb, 2, cblk), lambda i, c: (i, 0, c)),
        ],
        out_shape=[
            jax.ShapeDtypeStruct((b, seq, d_rnn), BF16),
            jax.ShapeDtypeStruct((b, 2, d_rnn), F32),
        ],
        scratch_shapes=[
            pltpu.VMEM((nb, seq, d), BF16),
            pltpu.VMEM((n_slab, 2 * (seq + 2 * SUBLANES), LANES), F32),
            pltpu.VMEM((nb, seq, cblk), F32),
            slab, slab, slab, slab,
            slab, slab,
        ],
        compiler_params=pltpu.CompilerParams(
            dimension_semantics=("arbitrary", "arbitrary"), vmem_limit_bytes=VMEM_LIMIT),
        name="rg_core",
    )(x, mod, gn, w_in, w_in, conv_w, conv_b, wg, lam, h0)


def _rope_pairs(v, tab):
    p = v * tab
    return p + pltpu.roll(p, QK_ROPE, axis=1)


def _mlaproj_kernel(x_ref, mod_ref, gn_ref, wd_ref, gq_ref, gkv_ref, wuq_ref, wk_ref, wv_ref, tab_ref,
                    tabt_ref, q_ref, k_ref, v_ref, *cache_refs, q_scale):
    mod = mod_ref[0, 0]
    h = _modnorm(x_ref[...], gn_ref[...], mod[0:1], mod[1:2]).astype(BF16)
    proj = _dot(h, wd_ref[...])
    cq = _rms(proj[:, :Q_LORA], gq_ref[...])
    ckv = _rms(proj[:, Q_LORA:Q_LORA + KV_LORA], gkv_ref[...])
    if cache_refs:
        ckv_ref, kpe_ref = cache_refs
        ckv_ref[...] = ckv
        kpe_ref[...] = proj[:, Q_LORA + KV_LORA:Q_LORA + KV_LORA + QK_ROPE]
    kp2 = proj[:, Q_LORA + KV_LORA + LANES:]
    lane = lax.broadcasted_iota(jnp.int32, kp2.shape, 1)
    kr = jnp.where(lane < QK_ROPE, _rope_pairs(kp2, tab_ref[...]), 0.0).astype(BF16)
    qt = _dot_nt(wuq_ref[...], cq.astype(BF16))
    ckvb = ckv.astype(BF16)
    kn = _dot(ckvb, wk_ref[...])
    vt = _dot_nt(wv_ref[...], ckvb).astype(BF16)
    nbt, _, tb = q_ref.shape
    for bi in range(nbt):
        v_ref[bi] = vt[:, bi * tb:(bi + 1) * tb]
    half = QK_ROPE // 2
    cos = tabt_ref[0:half, :] * q_scale
    sin = tabt_ref[half:, :] * q_scale
    zeros = jnp.zeros((HEAD_SLOT - QK_NOPE - QK_ROPE, tb), BF16)
    for hd in range(N_HEADS):
        lo = hd * HEAD_SLOT
        src = hd * (QK_NOPE + QK_ROPE)
        qn = (qt[src:src + QK_NOPE] * q_scale).astype(BF16)
        qe = qt[src + QK_NOPE:src + QK_NOPE + half]
        qo = qt[src + QK_NOPE + half:src + QK_NOPE + QK_ROPE]
        re = (qe * cos - qo * sin).astype(BF16)
        ro = (qe * sin + qo * cos).astype(BF16)
        for bi in range(nbt):
            cols = slice(bi * tb, (bi + 1) * tb)
            q_ref[bi, lo:lo + QK_NOPE, :] = qn[:, cols]
            q_ref[bi, lo + QK_NOPE:lo + QK_NOPE + half, :] = re[:, cols]
            q_ref[bi, lo + QK_NOPE + half:lo + QK_NOPE + QK_ROPE, :] = ro[:, cols]
            q_ref[bi, lo + QK_NOPE + QK_ROPE:lo + HEAD_SLOT, :] = zeros
        k_ref[:, lo:lo + QK_NOPE] = kn[:, hd * QK_NOPE:(hd + 1) * QK_NOPE].astype(BF16)
        k_ref[:, lo + QK_NOPE:lo + HEAD_SLOT] = kr


def _mlaproj_call(x, mod, layer, row_of_tile, gn, wd, gq, gkv, wuq_t, wk, wv_t, tab, tab_t, tab_tile, *,
                  seq, tm, emit_cache):
    ntok, d = x.shape
    const = dict(pipeline_mode=pl.Buffered(1))
    if tm <= seq:
        nt = seq // tm
        fm_block, fm_index = (1, tm), (lambda i: (i // nt, 0, i % nt))
    else:
        fm_block, fm_index = (tm // seq, seq), (lambda i: (i, 0, 0))
    out_specs = [
        pl.BlockSpec((fm_block[0], N_HEADS * HEAD_SLOT, fm_block[1]), fm_index),
        pl.BlockSpec((tm, N_HEADS * HEAD_SLOT), lambda i: (i, 0)),
        pl.BlockSpec((fm_block[0], N_HEADS * V_DIM, fm_block[1]), fm_index),
    ]
    out_shape = [
        jax.ShapeDtypeStruct((ntok // seq, N_HEADS * HEAD_SLOT, seq), BF16),
        jax.ShapeDtypeStruct((ntok, N_HEADS * HEAD_SLOT), BF16),
        jax.ShapeDtypeStruct((ntok // seq, N_HEADS * V_DIM, seq), BF16),
    ]
    if emit_cache:
        out_specs += [pl.BlockSpec((tm, KV_LORA), lambda i: (i, 0)),
                      pl.BlockSpec((tm, QK_ROPE), lambda i: (i, 0))]
        out_shape += [jax.ShapeDtypeStruct((ntok, KV_LORA), F32),
                      jax.ShapeDtypeStruct((ntok, QK_ROPE), F32)]
    return pl.pallas_call(
        functools.partial(_mlaproj_kernel, q_scale=float((QK_NOPE + QK_ROPE) ** -0.5 * LOG2_E)),
        grid=(ntok // tm,),
        in_specs=[
            pl.BlockSpec((tm, d), lambda i: (i, 0)),
            pl.BlockSpec((1, 1, 6, d), lambda i: (layer, row_of_tile(i), 0, 0)),
            pl.BlockSpec((1, d), lambda i: (0, 0)),
            pl.BlockSpec(wd.shape, lambda i: (0, 0), **const),
            pl.BlockSpec((1, Q_LORA), lambda i: (0, 0)),
            pl.BlockSpec((1, KV_LORA), lambda i: (0, 0)),
            pl.BlockSpec(wuq_t.shape, lambda i: (0, 0), **const),
            pl.BlockSpec(wk.shape, lambda i: (0, 0), **const),
            pl.BlockSpec(wv_t.shape, lambda i: (0, 0), **const),
            pl.BlockSpec((tm, LANES), lambda i: (tab_tile(i), 0)),
            pl.BlockSpec((QK_ROPE, tm), lambda i: (0, tab_tile(i))),
        ],
        out_specs=out_specs,
        out_shape=out_shape,
        compiler_params=pltpu.CompilerParams(
            dimension_semantics=("arbitrary",), vmem_limit_bytes=VMEM_LIMIT),
        name="mla_proj",
    )(x, mod, gn, wd, gq, gkv, wuq_t, wk, wv_t, tab, tab_t)


def _cachekv_kernel(ckv_ref, kpe_ref, wk_ref, wv_ref, k_ref, v_ref):
    ckvb = ckv_ref[...].astype(BF16)
    kn = _dot(ckvb, wk_ref[...])
    v_ref[0] = _dot_nt(wv_ref[...], ckvb).astype(BF16)
    kp = kpe_ref[...].astype(BF16)
    for hd in range(N_HEADS):
        lo = hd * HEAD_SLOT
        k_ref[:, lo:lo + QK_NOPE] = kn[:, hd * QK_NOPE:(hd + 1) * QK_NOPE].astype(BF16)
        k_ref[:, lo + QK_NOPE:lo + HEAD_SLOT] = kp


def _cachekv_call(ckv, kpe_padded, wk, wv_t, *, past):
    ntok = ckv.shape[0]
    return pl.pallas_call(
        _cachekv_kernel,
        grid=(ntok // past,),
        in_specs=[
            pl.BlockSpec((past, KV_LORA), lambda i: (i, 0)),
            pl.BlockSpec((past, LANES), lambda i: (i, 0)),
            pl.BlockSpec(wk.shape, lambda i: (0, 0)),
            pl.BlockSpec(wv_t.shape, lambda i: (0, 0)),
        ],
        out_specs=[
            pl.BlockSpec((past, N_HEADS * HEAD_SLOT), lambda i: (i, 0)),
            pl.BlockSpec((1, N_HEADS * V_DIM, past), lambda i: (i, 0, 0)),
        ],
        out_shape=[
            jax.ShapeDtypeStruct((ntok, N_HEADS * HEAD_SLOT), BF16),
            jax.ShapeDtypeStruct((ntok // past, N_HEADS * V_DIM, past), BF16),
        ],
        compiler_params=pltpu.CompilerParams(
            dimension_semantics=("arbitrary",), vmem_limit_bytes=VMEM_LIMIT),
        name="mla_cache_kv",
    )(ckv, kpe_padded, wk, wv_t)


def _attn_kernel(q_ref, k_ref, v_ref, *rest):
    if len(rest) == 4:
        kc_ref, vc_ref, o_ref, s_scr = rest
    else:
        kc_ref = vc_ref = None
        o_ref, s_scr = rest
    kchunk = min(512, k_ref.shape[1])
    chunks = [(k_ref, v_ref, r, r, kchunk) for r in range(0, k_ref.shape[1], kchunk)]
    if kc_ref is not None:
        cchunk = min(kchunk, kc_ref.shape[1])
        chunks += [(kc_ref, vc_ref, r, k_ref.shape[1] + r, cchunk) for r in range(0, kc_ref.shape[1], cchunk)]

    tq = s_scr.shape[2]
    n_sub = q_ref.shape[2] // tq
    n_item = q_ref.shape[0] * n_sub * N_HEADS

    def item_index(it):
        return it // (n_sub * N_HEADS), (it // N_HEADS) % n_sub, it % N_HEADS

    def score_chunk(it, chunk):
        kref, _, r, srow, n = chunk
        bi, sub, hd = item_index(it)
        slot = slice(hd * HEAD_SLOT, (hd + 1) * HEAD_SLOT)
        s = _dot(kref[bi, r:r + n, slot], q_ref[bi, slot, sub * tq:(sub + 1) * tq])
        s_scr[it % n_buf, srow:srow + n, :] = s
        return jnp.max(s, axis=0, keepdims=True)

    def value_chunk(it, chunk, m):
        _, vref, r, srow, n = chunk
        bi, _, hd = item_index(it)
        p = jnp.exp2(s_scr[it % n_buf, srow:srow + n, :] - m)
        ov = _dot(vref[bi, hd * V_DIM:(hd + 1) * V_DIM, r:r + n], p.astype(BF16))
        return jnp.sum(p, axis=0, keepdims=True), ov

    def running_max(acc, new):
        return new if acc is None else jnp.maximum(acc, new)

    n_buf = s_scr.shape[0]
    nc = len(chunks)
    ahead = nc + min(nc, 3)
    assert n_buf == 3 and nc < ahead <= 2 * nc
    total = n_item * nc
    maxes = {}

    def issue_scores(i):
        it, c = divmod(i, nc)
        maxes[it] = running_max(maxes.get(it), score_chunk(it, chunks[c]))

    for i in range(ahead):
        issue_scores(i)
    l, ov = None, None
    for j in range(total):
        if j + ahead < total:
            issue_scores(j + ahead)
        it, c = divmod(j, nc)
        l_c, ov_c = value_chunk(it, chunks[c], maxes[it])
        l = l_c if c == 0 else l + l_c
        ov = ov_c if c == 0 else ov + ov_c
        if c == nc - 1:
            bi, sub, hd = item_index(it)
            row0 = (bi * n_sub + sub) * tq
            o_ref[row0:row0 + tq, hd * V_DIM:(hd + 1) * V_DIM] = (ov * (1.0 / l)).T.astype(BF16)


def _attn_call(q_t, k, v_t, kc, vc_t, *, tq, nsub, nb):
    batch, kw, seq = q_t.shape
    ntok = batch * seq
    tstep = tq * nsub
    nq = seq // tstep
    assert nb == 1 or nq == 1
    vw = N_HEADS * V_DIM
    args = [q_t, k.reshape(batch, seq, kw), v_t]
    in_specs = [
        pl.BlockSpec((nb, kw, tstep), lambda b, i: (b, 0, i)),
        pl.BlockSpec((nb, seq, kw), lambda b, i: (b, 0, 0)),
        pl.BlockSpec((nb, vw, seq), lambda b, i: (b, 0, 0)),
    ]
    past = 0
    if kc is not None:
        past = kc.shape[0] // batch
        args += [kc.reshape(batch, past, kw), vc_t]
        in_specs += [pl.BlockSpec((nb, past, kw), lambda b, i: (b, 0, 0)),
                     pl.BlockSpec((nb, vw, past), lambda b, i: (b, 0, 0))]
    return pl.pallas_call(
        _attn_kernel,
        grid=(batch // nb, nq),
        in_specs=in_specs,
        out_specs=pl.BlockSpec((nb * tstep, vw), lambda b, i: (b * nq + i, 0)),
        out_shape=jax.ShapeDtypeStruct((ntok, vw), BF16),
        scratch_shapes=[pltpu.VMEM((3, seq + past, tq), F32)],
        compiler_params=pltpu.CompilerParams(
            dimension_semantics=("arbitrary", "arbitrary"), vmem_limit_bytes=VMEM_LIMIT),
        name="mla_attn",
    )(*args)


def _pairs(w):
    wp = jnp.swapaxes(w.reshape(w.shape[:-1] + (w.shape[-1] // 2, 2)), -1, -2)
    return wp[..., 0, :], wp[..., 1, :]


def _deinterleave(w):
    x, y = _pairs(w)
    return jnp.concatenate([x, y], axis=-1)


def _deinterleaved_partner(w):
    x, y = _pairs(w)
    return jnp.concatenate([-y, x], axis=-1)


def _rope_tables(n):
    rows = n // GRID_W
    row = jnp.repeat(jnp.arange(rows, dtype=F32), GRID_W)
    col = jnp.tile(jnp.arange(GRID_W, dtype=F32), rows)
    half = QK_ROPE // 2
    inv = ROPE_THETA ** (-jnp.arange(0, half, 2, dtype=F32) / half)
    ang = jnp.concatenate([row[:, None] * inv, col[:, None] * inv], axis=-1)
    cos, sin = jnp.cos(ang), jnp.sin(ang)
    return jnp.concatenate([cos, cos, sin, sin], axis=-1), jnp.concatenate([cos, sin], axis=-1).T


def _identity_rope_tables(n):
    half = QK_ROPE // 2
    one, zero = jnp.ones((n, half), F32), jnp.zeros((n, half), F32)
    return jnp.concatenate([one, one, zero, zero], axis=-1), jnp.concatenate([one, zero], axis=-1).T


def kernel(x_prompt, x_sample, state_rglru, cache_ckv, cache_kpe, c, c_ctx, ada_w, ada_b, norm_mix, norm_mlp, mlp_w1, mlp_w2, rg_w_in, rg_conv_w, rg_conv_b, rg_wa, rg_ba, rg_wx, rg_bx, rg_lambda, rg_w_out, mla_w_dqkv, mla_norm_q, mla_norm_kv, mla_w_uq, mla_w_ukv, mla_w_o, final_norm):
    batch, seq, d = x_prompt.shape
    dec_batch, dec_seq, _ = x_sample.shape
    depth = ada_w.shape[0]
    d_rnn = rg_lambda.shape[-1]
    past = cache_ckv.shape[2]
    assert dec_batch == COND_ROWS - SAMPLE_ROW0

    cond = jnp.concatenate(
        [jnp.broadcast_to(c_ctx[None, :], (SAMPLE_ROW0, d)), c], axis=0)
    mod = _ada_call(cond, ada_w, ada_b).reshape(depth, COND_ROWS, 6, d)

    w1 = mlp_w1.astype(BF16)
    w2 = mlp_w2.astype(BF16)
    w_in = jnp.concatenate([rg_w_in[..., :d_rnn], 0.5 * rg_w_in[..., d_rnn:]], axis=-1).astype(BF16)
    w_out = rg_w_out.astype(BF16)
    w_o = mla_w_o.astype(BF16)
    n_rg, n_mla = rg_w_in.shape[0], mla_w_dqkv.shape[0]
    nblk = d_rnn // RG_BS
    wg = (0.5 * jnp.concatenate([rg_wa[:, 0], rg_wa[:, 1], rg_wx[:, 0], rg_wx[:, 1]], axis=-1)).astype(BF16)
    bg = 0.5 * jnp.concatenate(
        [rg_ba[:, 0].reshape(n_rg, nblk, 1, RG_BS), rg_ba[:, 1].reshape(n_rg, nblk, 1, RG_BS),
         rg_bx[:, 0].reshape(n_rg, nblk, 1, RG_BS), rg_bx[:, 1].reshape(n_rg, nblk, 1, RG_BS)], axis=-1)
    bg_hi = bg.astype(BF16)
    bg_lo = (bg - bg_hi.astype(F32)).astype(BF16)
    wg = jnp.concatenate(
        [wg, bg_hi, bg_lo, jnp.zeros((n_rg, nblk, RG_BS - 2, 4 * RG_BS), BF16)], axis=2)
    lo = Q_LORA + KV_LORA
    wd, wuq, wk, wv = [], [], [], []
    for j in range(n_mla):
        w = mla_w_dqkv[j]
        w_pe = w[:, lo:]
        wd.append(jnp.concatenate(
            [w, jnp.zeros((d, LANES - QK_ROPE), F32), _deinterleave(w_pe), _deinterleaved_partner(w_pe)],
            axis=1).astype(BF16))
        wq = mla_w_uq[j].reshape(Q_LORA, N_HEADS, QK_NOPE + QK_ROPE)
        wq = jnp.concatenate([wq[..., :QK_NOPE], _deinterleave(wq[..., QK_NOPE:])], axis=-1)
        wuq.append(wq.reshape(Q_LORA, N_HEADS * (QK_NOPE + QK_ROPE)).T.astype(BF16))
        wkv = mla_w_ukv[j].reshape(KV_LORA, N_HEADS, QK_NOPE + V_DIM)
        wk.append(wkv[..., :QK_NOPE].reshape(KV_LORA, N_HEADS * QK_NOPE).astype(BF16))
        wv.append(wkv[..., QK_NOPE:].reshape(KV_LORA, N_HEADS * V_DIM).T.astype(BF16))

    fn = final_norm.reshape(1, d)
    tabs_sample = _rope_tables(dec_seq)
    tm = 512

    def run_stream(x, h0_all, cache, *, sample):
        b, t, _ = x.shape
        assert (not sample) or t % tm == 0
        tmp = tm
        rg_group = 1 if sample else 4
        if sample:
            row_of_batch = lambda i: SAMPLE_ROW0 + i
            row_of_tile = lambda i: SAMPLE_ROW0 + i // (t // tm)
            row_of_ptile = lambda i: SAMPLE_ROW0 + i // (t // tmp)
            (tab, tab_t), tab_tile = tabs_sample, (lambda i: i % (t // tmp))
        else:
            row_of_batch = lambda i: 0
            row_of_tile = lambda i: 0
            row_of_ptile = lambda i: 0
            (tab, tab_t), tab_tile = _identity_rope_tables(tmp), (lambda i: 0)
        xt = x.reshape(b * t, d)
        states, ckvs, kpes = [], [], []
        for l in range(depth):
            j = l // 2
            if l % 2 == 0:
                y, fin = _rg_call(xt.reshape(b, t, d), mod, l, row_of_batch, norm_mix[l].reshape(1, d),
                                  w_in, j, rg_conv_w[j], rg_conv_b[j].reshape(1, d_rnn), wg[j],
                                  rg_lambda[j], h0_all[:, j], cblk=256, nb=rg_group)
                states.append(fin)
                y = y.reshape(b * t, d_rnn)
                wp = w_out
            else:
                outs = _mlaproj_call(xt, mod, l, row_of_ptile, norm_mix[l].reshape(1, d), wd[j],
                                     mla_norm_q[j].reshape(1, Q_LORA), mla_norm_kv[j].reshape(1, KV_LORA),
                                     wuq[j], wk[j], wv[j], tab, tab_t, tab_tile, seq=t, tm=tmp,
                                     emit_cache=not sample)
                q, k, v = outs[:3]
                if sample:
                    ckv_c, kpe_c = cache
                    kpe_p = jnp.pad(_deinterleave(kpe_c[:, j]).reshape(b * past, QK_ROPE),
                                    ((0, 0), (0, LANES - QK_ROPE)))
                    kc, vc = _cachekv_call(ckv_c[:, j].reshape(b * past, KV_LORA), kpe_p, wk[j], wv[j], past=past)
                else:
                    kc = vc = None
                    ckvs.append(outs[3].reshape(b, t, KV_LORA))
                    kpes.append(outs[4].reshape(b, t, QK_ROPE))
                y = _attn_call(q, k, v, kc, vc, tq=min(512, t), nsub=1, nb=1 if sample else 4)
                wp = w_o
            xt = _post_call(xt, y, mod, l, row_of_tile, norm_mlp[l].reshape(1, d), wp, j, w1, w2, fn,
                            tm=tm, final=(l == depth - 1))
        return xt.reshape(b, t, d), states, ckvs, kpes

    zeros_h0 = jnp.zeros((batch, n_rg, 2, d_rnn), F32)
    y_prompt, states, ckvs, kpes = run_stream(x_prompt, zeros_h0, None, sample=False)
    y_sample, _, _, _ = run_stream(x_sample, state_rglru, (cache_ckv, cache_kpe), sample=True)
    return (y_prompt, y_sample, jnp.stack(states, axis=1), jnp.stack(ckvs, axis=1), jnp.stack(kpes, axis=1))
```

```python
import functools

import jax
import jax.numpy as jnp
from jax import lax
from jax.experimental import pallas as pl
from jax.experimental.pallas import tpu as pltpu

F32 = jnp.float32
BF16 = jnp.bfloat16

EPS = 1e-6
LOG2_E = 1.4426950408889634
RG_C = 8.0
RG_BS = 128
N_HEADS = 8
QK_NOPE = 128
QK_ROPE = 64
V_DIM = 128
Q_LORA = 512
KV_LORA = 256
GRID_W = 64
ROPE_THETA = 10000.0
HEAD_SLOT = 256

LANES = 128
SUBLANES = 8
VMEM_LIMIT = 56 * 1024 * 1024

COND_ROWS = 16
SAMPLE_ROW0 = 8


def _dot(a, b):
    return jnp.dot(a, b, preferred_element_type=F32)


def _dot_nt(a, b):
    return lax.dot_general(a, b, (((1,), (1,)), ((), ())), preferred_element_type=F32)


def _rms(x, g):
    ms = jnp.mean(x * x, axis=-1, keepdims=True)
    return x * lax.rsqrt(ms + EPS) * g


def _modnorm(x, g, shift, scale):
    ms = jnp.mean(x * x, axis=-1, keepdims=True)
    return x * lax.rsqrt(ms + EPS) * (g * (1.0 + scale)) + shift


def _sigmoid(x):
    return 0.5 * jnp.tanh(0.5 * x) + 0.5


def _gelu_tanh_of_half(xh):
    c = 0.7978845608028654
    return xh + xh * jnp.tanh(xh * (2.0 * c + (8.0 * c * 0.044715) * (xh * xh)))


def _ada_kernel(cond_ref, w_ref, b_ref, o_ref):
    c = cond_ref[...]
    s = c * _sigmoid(c)
    rows = s.shape[0]
    s_hi = s.astype(BF16)
    s_lo = (s - s_hi.astype(F32)).astype(BF16)
    w = w_ref[0]
    w_hi = w.astype(BF16)
    w_lo = (w - w_hi.astype(F32)).astype(BF16)
    main = _dot(jnp.concatenate([s_hi, s_lo], axis=0), w_hi)
    o_ref[0] = main[:rows] + main[rows:] + _dot(s_hi, w_lo) + b_ref[0]


def _ada_call(cond, ada_w, ada_b):
    depth, d, n6 = ada_w.shape
    tn = 1536
    return pl.pallas_call(
        _ada_kernel,
        grid=(depth, n6 // tn),
        in_specs=[
            pl.BlockSpec((COND_ROWS, d), lambda l, n: (0, 0)),
            pl.BlockSpec((1, d, tn), lambda l, n: (l, 0, n)),
            pl.BlockSpec((1, 1, tn), lambda l, n: (l, 0, n)),
        ],
        out_specs=pl.BlockSpec((1, COND_ROWS, tn), lambda l, n: (l, 0, n)),
        out_shape=jax.ShapeDtypeStruct((depth, COND_ROWS, n6), F32),
        compiler_params=pltpu.CompilerParams(
            dimension_semantics=("arbitrary", "arbitrary"), vmem_limit_bytes=VMEM_LIMIT),
        name="ada",
    )(cond, ada_w, ada_b.reshape(depth, 1, n6))


def _post_kernel(x_ref, y_ref, mod_ref, gn_ref, wp_ref, w1_ref, w2_ref, fn_ref, o_ref, *,
                 ff_chunk, final):
    mod = mod_ref[0, 0]
    g1, sh2, sc2, g2 = mod[2:3], mod[3:4], mod[4:5], mod[5:6]
    x1 = x_ref[...] + g1 * _dot(y_ref[...], wp_ref[...])
    h = _modnorm(x1, gn_ref[...], sh2, sc2).astype(BF16)
    ff = w1_ref.shape[1]
    acc = jnp.zeros_like(x1)
    for c in range(ff // ff_chunk):
        a = jnp.maximum(_dot(h, w1_ref[:, c * ff_chunk:(c + 1) * ff_chunk]), 0.0)
        acc = acc + _dot((a * a).astype(BF16), w2_ref[c * ff_chunk:(c + 1) * ff_chunk, :])
    out = x1 + g2 * acc
    if final:
        out = _rms(out, fn_ref[...])
    o_ref[...] = out


def _post_call(x, y, mod, layer, row_of_tile, gn, wp, wp_index, w1, w2, fn, *, tm, final):
    ntok, d = x.shape
    ff = w1.shape[2]
    const = dict(pipeline_mode=pl.Buffered(1))
    return pl.pallas_call(
        functools.partial(_post_kernel, ff_chunk=1024, final=final),
        grid=(ntok // tm,),
        in_specs=[
            pl.BlockSpec((tm, d), lambda i: (i, 0)),
            pl.BlockSpec((tm, d), lambda i: (i, 0)),
            pl.BlockSpec((1, 1, 6, d), lambda i: (layer, row_of_tile(i), 0, 0)),
            pl.BlockSpec((1, d), lambda i: (0, 0)),
            pl.BlockSpec((None, d, d), lambda i: (wp_index, 0, 0), **const),
            pl.BlockSpec((None, d, ff), lambda i: (layer, 0, 0), **const),
            pl.BlockSpec((None, ff, d), lambda i: (layer, 0, 0), **const),
            pl.BlockSpec((1, d), lambda i: (0, 0)),
        ],
        out_specs=pl.BlockSpec((tm, d), lambda i: (i, 0)),
        out_shape=jax.ShapeDtypeStruct((ntok, d), F32),
        compiler_params=pltpu.CompilerParams(
            dimension_semantics=("arbitrary",), vmem_limit_bytes=VMEM_LIMIT),
        name="post",
    )(x, y, mod, gn, wp, w1, w2, fn)


def _rg_kernel(x_ref, mod_ref, gn_ref, wx_ref, wy_ref, cw_ref, cb_ref, wg_ref, lam_ref,
               h0_ref, y_ref, fin_ref,
               h_scr, xb_scr, yb_scr, af_scr, bf_scr, ab_scr, bb_scr, hf_scr, hb_scr, *,
               seq, seg, chunk):
    nb = x_ref.shape[0]
    n_lane_grp = wx_ref.shape[1] // LANES
    n_slab = nb * n_lane_grp
    n_chunk = seq // chunk
    pad = SUBLANES * seg - seq
    halo = SUBLANES

    @pl.when(pl.program_id(1) == 0)
    def _():
        mod = mod_ref[0, 0]
        for bi in range(nb):
            for c in range(n_chunk):
                rows = slice(c * chunk, (c + 1) * chunk)
                h_scr[bi, rows, :] = _modnorm(
                    x_ref[bi, rows, :], gn_ref[...], mod[0:1], mod[1:2]).astype(BF16)

    def xb_rows(t0, n):
        return pl.ds(2 * (halo + t0), n, stride=2)

    for sl in range(n_slab):
        xb_scr[sl, 0:2 * halo, :] = jnp.zeros((2 * halo, LANES), F32)
        xb_scr[sl, 2 * (halo + seq):2 * (2 * halo + seq), :] = jnp.zeros((2 * halo, LANES), F32)
        tail = slice(seq, seq + pad)
        af_scr[sl, tail, :] = jnp.ones((pad, LANES), F32)
        ab_scr[sl, tail, :] = jnp.ones((pad, LANES), F32)
        bf_scr[sl, tail, :] = jnp.zeros((pad, LANES), F32)
        bb_scr[sl, tail, :] = jnp.zeros((pad, LANES), F32)

    def project(bi, c):
        rows = slice(c * chunk, (c + 1) * chunk)
        hc = h_scr[bi, rows, :]
        ux = _dot(hc, wx_ref[...])
        for s in range(n_lane_grp):
            xb_scr[bi * n_lane_grp + s, xb_rows(c * chunk, chunk), :] = ux[:, s * LANES:(s + 1) * LANES]
        yb_scr[bi, rows, :] = _gelu_tanh_of_half(_dot(hc, wy_ref[...]))

    cw = cw_ref[...]
    lam = lam_ref[...]
    c1_all = (-0.5 * RG_C * LOG2_E) * (jnp.maximum(-lam, 0.0) + jnp.log1p(jnp.exp(-jnp.abs(lam))))
    lane = lax.broadcasted_iota(jnp.int32, (chunk, LANES), 1)
    bias_cols = jnp.where(lane < 2, 1.0, 0.0).astype(BF16)

    def gates(bi, c):
        rows = slice(c * chunk, (c + 1) * chunk)
        for s in range(n_lane_grp):
            sl = bi * n_lane_grp + s
            lanes = slice(s * LANES, (s + 1) * LANES)
            xs = cb_ref[:, lanes] + cw[0:1, lanes] * xb_scr[sl, xb_rows(c * chunk - 1, chunk), :]
            for k in range(1, 4):
                xs = xs + cw[k:k + 1, lanes] * xb_scr[sl, xb_rows(c * chunk - 1 + k, chunk), :]
            g = _dot(jnp.concatenate([xs.astype(BF16), bias_cols], axis=1), wg_ref[s])
            xh = 0.5 * xs
            for d, (a_scr, b_scr) in enumerate(((af_scr, bf_scr), (ab_scr, bb_scr))):
                c1 = c1_all[d:d + 1, lanes]
                tr = jnp.tanh(g[:, d * LANES:(d + 1) * LANES])
                ti = jnp.tanh(g[:, (2 + d) * LANES:(3 + d) * LANES])
                a = jnp.exp2(c1 * tr + c1)
                a_scr[sl, rows, :] = a
                v = 1.0 - a * a
                root = jnp.where(v > 0.0, v * lax.rsqrt(v), 0.0)
                b_scr[sl, rows, :] = root * (ti * xh + xh)

    for bi in range(nb):
        project(bi, 0)
        for c in range(1, n_chunk):
            project(bi, c)
            gates(bi, c - 1)
        gates(bi, n_chunk - 1)

    def seg_rows(j):
        return pl.ds(j, SUBLANES, stride=seg)

    def run_steps(step, init):
        unroll = 8
        main = seg // unroll

        def body(i, carry):
            for u in range(unroll):
                carry = step(i * unroll + u, carry)
            return carry

        carry = lax.fori_loop(0, main, body, init)
        for j in range(main * unroll, seg):
            carry = step(j, carry)
        return carry

    def pass1(j, carry):
        jb = seg - 1 - j
        out = []
        for sl in range(n_slab):
            pf, hf, pb, hb = carry[sl]
            a = af_scr[sl, seg_rows(j), :]
            b = bf_scr[sl, seg_rows(j), :]
            a2 = ab_scr[sl, seg_rows(jb), :]
            b2 = bb_scr[sl, seg_rows(jb), :]
            out.append((pf * a, a * hf + b, pb * a2, a2 * hb + b2))
        return tuple(out)

    one = jnp.ones((SUBLANES, LANES), F32)
    zero = jnp.zeros((SUBLANES, LANES), F32)
    totals = run_steps(pass1, tuple((one, zero, one, zero) for _ in range(n_slab)))

    rid = lax.broadcasted_iota(jnp.int32, (SUBLANES, LANES), 0)
    init_f, init_b = [], []
    for sl in range(n_slab):
        bi, s = divmod(sl, n_lane_grp)
        lanes = slice(s * LANES, (s + 1) * LANES)
        pf, hf, pb, hb = totals[sl]
        st = h0_ref[bi, 0:1, lanes]
        ini = zero
        for k in range(SUBLANES):
            ini = jnp.where(rid == k, st, ini)
            st = hf[k:k + 1] + pf[k:k + 1] * st
        init_f.append(ini)
        fin_ref[bi, 0:1, lanes] = st
        st = h0_ref[bi, 1:2, lanes]
        ini = zero
        for k in range(SUBLANES - 1, -1, -1):
            ini = jnp.where(rid == k, st, ini)
            st = hb[k:k + 1] + pb[k:k + 1] * st
        init_b.append(ini)
        fin_ref[bi, 1:2, lanes] = st

    def pass2(j, carry):
        jb = seg - 1 - j
        out = []
        for sl in range(n_slab):
            hf, hb = carry[sl]
            hf = af_scr[sl, seg_rows(j), :] * hf + bf_scr[sl, seg_rows(j), :]
            hf_scr[sl, seg_rows(j), :] = hf
            hb = ab_scr[sl, seg_rows(jb), :] * hb + bb_scr[sl, seg_rows(jb), :]
            hb_scr[sl, seg_rows(jb), :] = hb
            out.append((hf, hb))
        return tuple(out)

    run_steps(pass2, tuple(zip(init_f, init_b)))

    for sl in range(n_slab):
        bi, s = divmod(sl, n_lane_grp)
        lanes = slice(s * LANES, (s + 1) * LANES)
        for c in range(n_chunk):
            rows = slice(c * chunk, (c + 1) * chunk)
            y_ref[bi, rows, lanes] = (
                (hf_scr[sl, rows, :] + hb_scr[sl, rows, :]) * yb_scr[bi, rows, lanes]).astype(BF16)


def _odd_segment_length(seq):
    seg = -(-seq // SUBLANES)
    return seg if seg % 2 == 1 else seg + 1


def _rg_call(x, mod, layer, row_of_group, gn, w_in, w_index, conv_w, conv_b, wg, lam, h0, *, cblk, nb):
    b, seq, d = x.shape
    d_rnn = lam.shape[1]
    ncb = d_rnn // cblk
    n_lane_grp = cblk // LANES
    n_slab = nb * n_lane_grp
    seg = _odd_segment_length(seq)
    seq_pad = SUBLANES * seg
    chunk = min(256, seq // 2)
    slab = pltpu.VMEM((n_slab, seq_pad, LANES), F32)
    return pl.pallas_call(
        functools.partial(_rg_kernel, seq=seq, seg=seg, chunk=chunk),
        grid=(b // nb, ncb),
        in_specs=[
            pl.BlockSpec((nb, seq, d), lambda i, c: (i, 0, 0)),
            pl.BlockSpec((1, 1, 6, d), lambda i, c: (layer, row_of_group(i), 0, 0)),
            pl.BlockSpec((1, d), lambda i, c: (0, 0)),
            pl.BlockSpec((None, d, cblk), lambda i, c: (w_index, 0, c)),
            pl.BlockSpec((None, d, cblk), lambda i, c: (w_index, 0, ncb + c)),
            pl.BlockSpec((4, cblk), lambda i, c: (0, c)),
            pl.BlockSpec((1, cblk), lambda i, c: (0, c)),
            pl.BlockSpec((n_lane_grp, 2 * RG_BS, 4 * RG_BS), lambda i, c: (c, 0, 0)),
            pl.BlockSpec((2, cblk), lambda i, c: (0, c)),
            pl.BlockSpec((nb, 2, cblk), lambda i, c: (i, 0, c)),
        ],
        out_specs=[
            pl.BlockSpec((nb, seq, cblk), lambda i, c: (i, 0, c)),
            pl.BlockSpec((nb, 2, cblk), lambda i, c: (i, 0, c)),
        ],
        out_shape=[
            jax.ShapeDtypeStruct((b, seq, d_rnn), BF16),
            jax.ShapeDtypeStruct((b, 2, d_rnn), F32),
        ],
        scratch_shapes=[
            pltpu.VMEM((nb, seq, d), BF16),
            pltpu.VMEM((n_slab, 2 * (seq + 2 * SUBLANES), LANES), F32),
            pltpu.VMEM((nb, seq, cblk), F32),
            slab, slab, slab, slab,
            slab, slab,
        ],
        compiler_params=pltpu.CompilerParams(
            dimension_semantics=("arbitrary", "arbitrary"), vmem_limit_bytes=VMEM_LIMIT),
        name="rg_core",
    )(x, mod, gn, w_in, w_in, conv_w, conv_b, wg, lam, h0)


def _rope_pairs(v, tab):
    p = v * tab
    return p + pltpu.roll(p, QK_ROPE, axis=1)


def _mlaproj_kernel(x_ref, mod_ref, gn_ref, wd_ref, gq_ref, gkv_ref, wuq_ref, wk_ref, wv_ref, tab_ref,
                    tabt_ref, q_ref, k_ref, v_ref, *cache_refs, q_scale):
    mod = mod_ref[0, 0]
    h = _modnorm(x_ref[...], gn_ref[...], mod[0:1], mod[1:2]).astype(BF16)
    proj = _dot(h, wd_ref[...])
    cq = _rms(proj[:, :Q_LORA], gq_ref[...])
    ckv = _rms(proj[:, Q_LORA:Q_LORA + KV_LORA], gkv_ref[...])
    if cache_refs:
        ckv_ref, kpe_ref = cache_refs
        ckv_ref[...] = ckv
        kpe_ref[...] = proj[:, Q_LORA + KV_LORA:Q_LORA + KV_LORA + QK_ROPE]
    kp2 = proj[:, Q_LORA + KV_LORA + LANES:]
    lane = lax.broadcasted_iota(jnp.int32, kp2.shape, 1)
    kr = jnp.where(lane < QK_ROPE, _rope_pairs(kp2, tab_ref[...]), 0.0).astype(BF16)
    qt = _dot_nt(wuq_ref[...], cq.astype(BF16))
    ckvb = ckv.astype(BF16)
    kn = _dot(ckvb, wk_ref[...])
    vt = _dot_nt(wv_ref[...], ckvb).astype(BF16)
    nbt, _, tb = q_ref.shape
    for bi in range(nbt):
        v_ref[bi] = vt[:, bi * tb:(bi + 1) * tb]
    half = QK_ROPE // 2
    cos = tabt_ref[0:half, :] * q_scale
    sin = tabt_ref[half:, :] * q_scale
    zeros = jnp.zeros((HEAD_SLOT - QK_NOPE - QK_ROPE, tb), BF16)
    for hd in range(N_HEADS):
        lo = hd * HEAD_SLOT
        src = hd * (QK_NOPE + QK_ROPE)
        qn = (qt[src:src + QK_NOPE] * q_scale).astype(BF16)
        qe = qt[src + QK_NOPE:src + QK_NOPE + half]
        qo = qt[src + QK_NOPE + half:src + QK_NOPE + QK_ROPE]
        re = (qe * cos - qo * sin).astype(BF16)
        ro = (qe * sin + qo * cos).astype(BF16)
        for bi in range(nbt):
            cols = slice(bi * tb, (bi + 1) * tb)
            q_ref[bi, lo:lo + QK_NOPE, :] = qn[:, cols]
            q_ref[bi, lo + QK_NOPE:lo + QK_NOPE + half, :] = re[:, cols]
            q_ref[bi, lo + QK_NOPE + half:lo + QK_NOPE + QK_ROPE, :] = ro[:, cols]
            q_ref[bi, lo + QK_NOPE + QK_ROPE:lo + HEAD_SLOT, :] = zeros
        k_ref[:, lo:lo + QK_NOPE] = kn[:, hd * QK_NOPE:(hd + 1) * QK_NOPE].astype(BF16)
        k_ref[:, lo + QK_NOPE:lo + HEAD_SLOT] = kr


def _mlaproj_call(x, mod, layer, row_of_tile, gn, wd, gq, gkv, wuq_t, wk, wv_t, tab, tab_t, tab_tile, *,
                  seq, tm, emit_cache):
    ntok, d = x.shape
    const = dict(pipeline_mode=pl.Buffered(1))
    if tm <= seq:
        nt = seq // tm
        fm_block, fm_index = (1, tm), (lambda i: (i // nt, 0, i % nt))
    else:
        fm_block, fm_index = (tm // seq, seq), (lambda i: (i, 0, 0))
    out_specs = [
        pl.BlockSpec((fm_block[0], N_HEADS * HEAD_SLOT, fm_block[1]), fm_index),
        pl.BlockSpec((tm, N_HEADS * HEAD_SLOT), lambda i: (i, 0)),
        pl.BlockSpec((fm_block[0], N_HEADS * V_DIM, fm_block[1]), fm_index),
    ]
    out_shape = [
        jax.ShapeDtypeStruct((ntok // seq, N_HEADS * HEAD_SLOT, seq), BF16),
        jax.ShapeDtypeStruct((ntok, N_HEADS * HEAD_SLOT), BF16),
        jax.ShapeDtypeStruct((ntok // seq, N_HEADS * V_DIM, seq), BF16),
    ]
    if emit_cache:
        out_specs += [pl.BlockSpec((tm, KV_LORA), lambda i: (i, 0)),
                      pl.BlockSpec((tm, QK_ROPE), lambda i: (i, 0))]
        out_shape += [jax.ShapeDtypeStruct((ntok, KV_LORA), F32),
                      jax.ShapeDtypeStruct((ntok, QK_ROPE), F32)]
    return pl.pallas_call(
        functools.partial(_mlaproj_kernel, q_scale=float((QK_NOPE + QK_ROPE) ** -0.5 * LOG2_E)),
        grid=(ntok // tm,),
        in_specs=[
            pl.BlockSpec((tm, d), lambda i: (i, 0)),
            pl.BlockSpec((1, 1, 6, d), lambda i: (layer, row_of_tile(i), 0, 0)),
            pl.BlockSpec((1, d), lambda i: (0, 0)),
            pl.BlockSpec(wd.shape, lambda i: (0, 0), **const),
            pl.BlockSpec((1, Q_LORA), lambda i: (0, 0)),
            pl.BlockSpec((1, KV_LORA), lambda i: (0, 0)),
            pl.BlockSpec(wuq_t.shape, lambda i: (0, 0), **const),
            pl.BlockSpec(wk.shape, lambda i: (0, 0), **const),
            pl.BlockSpec(wv_t.shape, lambda i: (0, 0), **const),
            pl.BlockSpec((tm, LANES), lambda i: (tab_tile(i), 0)),
            pl.BlockSpec((QK_ROPE, tm), lambda i: (0, tab_tile(i))),
        ],
        out_specs=out_specs,
        out_shape=out_shape,
        compiler_params=pltpu.CompilerParams(
            dimension_semantics=("arbitrary",), vmem_limit_bytes=VMEM_LIMIT),
        name="mla_proj",
    )(x, mod, gn, wd, gq, gkv, wuq_t, wk, wv_t, tab, tab_t)


def _attn_kernel(q_ref, k_ref, v_ref, *rest):
    if len(rest) == 8:
        ckv_ref, kpe_ref, wk_ref, wv_ref, o_ref, s_scr, kc_ref, vc_ref = rest

        @pl.when(pl.program_id(1) == 0)
        def _():
            for bi in range(ckv_ref.shape[0]):
                ckvb = ckv_ref[bi].astype(BF16)
                kn = _dot(ckvb, wk_ref[...])
                vc_ref[bi] = _dot_nt(wv_ref[...], ckvb).astype(BF16)
                kp = kpe_ref[bi].astype(BF16)
                for hd in range(N_HEADS):
                    lo = hd * HEAD_SLOT
                    kc_ref[bi, :, lo:lo + QK_NOPE] = kn[:, hd * QK_NOPE:(hd + 1) * QK_NOPE].astype(BF16)
                    kc_ref[bi, :, lo + QK_NOPE:lo + HEAD_SLOT] = kp
    else:
        kc_ref = vc_ref = None
        o_ref, s_scr = rest
    kchunk = min(512, k_ref.shape[1])
    chunks = [(k_ref, v_ref, r, r, kchunk) for r in range(0, k_ref.shape[1], kchunk)]
    if kc_ref is not None:
        cchunk = min(kchunk, kc_ref.shape[1])
        chunks += [(kc_ref, vc_ref, r, k_ref.shape[1] + r, cchunk) for r in range(0, kc_ref.shape[1], cchunk)]

    tq = s_scr.shape[2]
    n_sub = q_ref.shape[2] // tq
    n_item = q_ref.shape[0] * n_sub * N_HEADS

    def item_index(it):
        return it // (n_sub * N_HEADS), (it // N_HEADS) % n_sub, it % N_HEADS

    def score_chunk(it, chunk):
        kref, _, r, srow, n = chunk
        bi, sub, hd = item_index(it)
        slot = slice(hd * HEAD_SLOT, (hd + 1) * HEAD_SLOT)
        s = _dot(kref[bi, r:r + n, slot], q_ref[bi, slot, sub * tq:(sub + 1) * tq])
        s_scr[it % n_buf, srow:srow + n, :] = s
        return jnp.max(s, axis=0, keepdims=True)

    def value_chunk(it, chunk, m):
        _, vref, r, srow, n = chunk
        bi, _, hd = item_index(it)
        p = jnp.exp2(s_scr[it % n_buf, srow:srow + n, :] - m)
        ov = _dot(vref[bi, hd * V_DIM:(hd + 1) * V_DIM, r:r + n], p.astype(BF16))
        return jnp.sum(p, axis=0, keepdims=True), ov

    def running_max(acc, new):
        return new if acc is None else jnp.maximum(acc, new)

    n_buf = s_scr.shape[0]
    nc = len(chunks)
    ahead = nc + min(nc, 3)
    assert n_buf == 3 and nc < ahead <= 2 * nc
    total = n_item * nc
    maxes = {}

    def issue_scores(i):
        it, c = divmod(i, nc)
        maxes[it] = running_max(maxes.get(it), score_chunk(it, chunks[c]))

    for i in range(ahead):
        issue_scores(i)
    l, ov = None, None
    for j in range(total):
        if j + ahead < total:
            issue_scores(j + ahead)
        it, c = divmod(j, nc)
        l_c, ov_c = value_chunk(it, chunks[c], maxes[it])
        l = l_c if c == 0 else l + l_c
        ov = ov_c if c == 0 else ov + ov_c
        if c == nc - 1:
            bi, sub, hd = item_index(it)
            row0 = (bi * n_sub + sub) * tq
            o_ref[row0:row0 + tq, hd * V_DIM:(hd + 1) * V_DIM] = (ov * (1.0 / l)).T.astype(BF16)


def _attn_call(q_t, k, v_t, cache, *, tq, nsub, nb):
    batch, kw, seq = q_t.shape
    ntok = batch * seq
    tstep = tq * nsub
    nq = seq // tstep
    assert nb == 1 or nq == 1
    vw = N_HEADS * V_DIM
    args = [q_t, k.reshape(batch, seq, kw), v_t]
    in_specs = [
        pl.BlockSpec((nb, kw, tstep), lambda b, i: (b, 0, i)),
        pl.BlockSpec((nb, seq, kw), lambda b, i: (b, 0, 0)),
        pl.BlockSpec((nb, vw, seq), lambda b, i: (b, 0, 0)),
    ]
    past = 0
    scratch = []
    if cache is not None:
        ckv_c, kpe_c, wk, wv_t = cache
        past = ckv_c.shape[1]
        args += [ckv_c, kpe_c, wk, wv_t]
        in_specs += [pl.BlockSpec((nb, past, KV_LORA), lambda b, i: (b, 0, 0)),
                     pl.BlockSpec((nb, past, LANES), lambda b, i: (b, 0, 0)),
                     pl.BlockSpec(wk.shape, lambda b, i: (0, 0)),
                     pl.BlockSpec(wv_t.shape, lambda b, i: (0, 0))]
        scratch = [pltpu.VMEM((nb, past, kw), BF16), pltpu.VMEM((nb, vw, past), BF16)]
    return pl.pallas_call(
        _attn_kernel,
        grid=(batch // nb, nq),
        in_specs=in_specs,
        out_specs=pl.BlockSpec((nb * tstep, vw), lambda b, i: (b * nq + i, 0)),
        out_shape=jax.ShapeDtypeStruct((ntok, vw), BF16),
        scratch_shapes=[pltpu.VMEM((3, seq + past, tq), F32)] + scratch,
        compiler_params=pltpu.CompilerParams(
            dimension_semantics=("arbitrary", "arbitrary"), vmem_limit_bytes=VMEM_LIMIT),
        name="mla_attn",
    )(*args)


def _pairs(w):
    wp = jnp.swapaxes(w.reshape(w.shape[:-1] + (w.shape[-1] // 2, 2)), -1, -2)
    return wp[..., 0, :], wp[..., 1, :]


def _deinterleave(w):
    x, y = _pairs(w)
    return jnp.concatenate([x, y], axis=-1)


def _deinterleaved_partner(w):
    x, y = _pairs(w)
    return jnp.concatenate([-y, x], axis=-1)


def _rope_tables(n):
    rows = n // GRID_W
    row = jnp.repeat(jnp.arange(rows, dtype=F32), GRID_W)
    col = jnp.tile(jnp.arange(GRID_W, dtype=F32), rows)
    half = QK_ROPE // 2
    inv = ROPE_THETA ** (-jnp.arange(0, half, 2, dtype=F32) / half)
    ang = jnp.concatenate([row[:, None] * inv, col[:, None] * inv], axis=-1)
    cos, sin = jnp.cos(ang), jnp.sin(ang)
    return jnp.concatenate([cos, cos, sin, sin], axis=-1), jnp.concatenate([cos, sin], axis=-1).T


def _identity_rope_tables(n):
    half = QK_ROPE // 2
    one, zero = jnp.ones((n, half), F32), jnp.zeros((n, half), F32)
    return jnp.concatenate([one, one, zero, zero], axis=-1), jnp.concatenate([one, zero], axis=-1).T


def kernel(x_prompt, x_sample, state_rglru, cache_ckv, cache_kpe, c, c_ctx, ada_w, ada_b, norm_mix, norm_mlp, mlp_w1, mlp_w2, rg_w_in, rg_conv_w, rg_conv_b, rg_wa, rg_ba, rg_wx, rg_bx, rg_lambda, rg_w_out, mla_w_dqkv, mla_norm_q, mla_norm_kv, mla_w_uq, mla_w_ukv, mla_w_o, final_norm):
    batch, seq, d = x_prompt.shape
    dec_batch, dec_seq, _ = x_sample.shape
    depth = ada_w.shape[0]
    d_rnn = rg_lambda.shape[-1]
    past = cache_ckv.shape[2]
    assert dec_batch == COND_ROWS - SAMPLE_ROW0

    cond = jnp.concatenate(
        [jnp.broadcast_to(c_ctx[None, :], (SAMPLE_ROW0, d)), c], axis=0)
    mod = _ada_call(cond, ada_w, ada_b).reshape(depth, COND_ROWS, 6, d)

    w1 = mlp_w1.astype(BF16)
    w2 = mlp_w2.astype(BF16)
    w_in = jnp.concatenate([rg_w_in[..., :d_rnn], 0.5 * rg_w_in[..., d_rnn:]], axis=-1).astype(BF16)
    w_out = rg_w_out.astype(BF16)
    w_o = mla_w_o.astype(BF16)
    n_rg, n_mla = rg_w_in.shape[0], mla_w_dqkv.shape[0]
    nblk = d_rnn // RG_BS
    wg = (0.5 * jnp.concatenate([rg_wa[:, 0], rg_wa[:, 1], rg_wx[:, 0], rg_wx[:, 1]], axis=-1)).astype(BF16)
    bg = 0.5 * jnp.concatenate(
        [rg_ba[:, 0].reshape(n_rg, nblk, 1, RG_BS), rg_ba[:, 1].reshape(n_rg, nblk, 1, RG_BS),
         rg_bx[:, 0].reshape(n_rg, nblk, 1, RG_BS), rg_bx[:, 1].reshape(n_rg, nblk, 1, RG_BS)], axis=-1)
    bg_hi = bg.astype(BF16)
    bg_lo = (bg - bg_hi.astype(F32)).astype(BF16)
    wg = jnp.concatenate(
        [wg, bg_hi, bg_lo, jnp.zeros((n_rg, nblk, RG_BS - 2, 4 * RG_BS), BF16)], axis=2)
    lo = Q_LORA + KV_LORA
    wd, wuq, wk, wv = [], [], [], []
    for j in range(n_mla):
        w = mla_w_dqkv[j]
        w_pe = w[:, lo:]
        wd.append(jnp.concatenate(
            [w, jnp.zeros((d, LANES - QK_ROPE), F32), _deinterleave(w_pe), _deinterleaved_partner(w_pe)],
            axis=1).astype(BF16))
        wq = mla_w_uq[j].reshape(Q_LORA, N_HEADS, QK_NOPE + QK_ROPE)
        wq = jnp.concatenate([wq[..., :QK_NOPE], _deinterleave(wq[..., QK_NOPE:])], axis=-1)
        wuq.append(wq.reshape(Q_LORA, N_HEADS * (QK_NOPE + QK_ROPE)).T.astype(BF16))
        wkv = mla_w_ukv[j].reshape(KV_LORA, N_HEADS, QK_NOPE + V_DIM)
        wk.append(wkv[..., :QK_NOPE].reshape(KV_LORA, N_HEADS * QK_NOPE).astype(BF16))
        wv.append(wkv[..., QK_NOPE:].reshape(KV_LORA, N_HEADS * V_DIM).T.astype(BF16))

    fn = final_norm.reshape(1, d)
    tabs_sample = _rope_tables(dec_seq)
    tm = 512

    def run_stream(x, h0_all, cache, *, sample):
        b, t, _ = x.shape
        assert (not sample) or t % tm == 0
        tmp = tm
        rg_group = 1 if sample else 4
        if sample:
            row_of_batch = lambda i: SAMPLE_ROW0 + i
            row_of_tile = lambda i: SAMPLE_ROW0 + i // (t // tm)
            row_of_ptile = lambda i: SAMPLE_ROW0 + i // (t // tmp)
            (tab, tab_t), tab_tile = tabs_sample, (lambda i: i % (t // tmp))
        else:
            row_of_batch = lambda i: 0
            row_of_tile = lambda i: 0
            row_of_ptile = lambda i: 0
            (tab, tab_t), tab_tile = _identity_rope_tables(tmp), (lambda i: 0)
        xt = x.reshape(b * t, d)
        states, ckvs, kpes = [], [], []
        for l in range(depth):
            j = l // 2
            if l % 2 == 0:
                y, fin = _rg_call(xt.reshape(b, t, d), mod, l, row_of_batch, norm_mix[l].reshape(1, d),
                                  w_in, j, rg_conv_w[j], rg_conv_b[j].reshape(1, d_rnn), wg[j],
                                  rg_lambda[j], h0_all[:, j], cblk=256, nb=rg_group)
                states.append(fin)
                y = y.reshape(b * t, d_rnn)
                wp = w_out
            else:
                outs = _mlaproj_call(xt, mod, l, row_of_ptile, norm_mix[l].reshape(1, d), wd[j],
                                     mla_norm_q[j].reshape(1, Q_LORA), mla_norm_kv[j].reshape(1, KV_LORA),
                                     wuq[j], wk[j], wv[j], tab, tab_t, tab_tile, seq=t, tm=tmp,
                                     emit_cache=not sample)
                q, k, v = outs[:3]
                if sample:
                    ckv_c, kpe_c = cache
                    kpe_p = jnp.pad(_deinterleave(kpe_c[:, j]), ((0, 0), (0, 0), (0, LANES - QK_ROPE)))
                    attn_cache = (ckv_c[:, j], kpe_p, wk[j], wv[j])
                else:
                    attn_cache = None
                    ckvs.append(outs[3].reshape(b, t, KV_LORA))
                    kpes.append(outs[4].reshape(b, t, QK_ROPE))
                y = _attn_call(q, k, v, attn_cache, tq=min(512, t), nsub=1, nb=1 if sample else 4)
                wp = w_o
            xt = _post_call(xt, y, mod, l, row_of_tile, norm_mlp[l].reshape(1, d), wp, j, w1, w2, fn,
                            tm=tm, final=(l == depth - 1))
        return xt.reshape(b, t, d), states, ckvs, kpes

    zeros_h0 = jnp.zeros((batch, n_rg, 2, d_rnn), F32)
    y_prompt, states, ckvs, kpes = run_stream(x_prompt, zeros_h0, None, sample=False)
    y_sample, _, _, _ = run_stream(x_sample, state_rglru, (cache_ckv, cache_kpe), sample=True)
    return (y_prompt, y_sample, jnp.stack(states, axis=1), jnp.stack(ckvs, axis=1), jnp.stack(kpes, axis=1))
```
